```python
import math
import jax
import jax.numpy as jnp
from jax import lax
import numpy as np

D_MODEL = 2048
BATCH = 4
SEQ = 2048
DEPTH = 2

PLE_DIM = 256
D_FF = 5632
RMS_EPS = 1e-6
D_SSM = D_MODEL // 2
SSM_GROUP = 16
SSM_GROUPS = D_SSM // SSM_GROUP
SSM_STATE = 64
D_ATTN = D_MODEL // 2
HEAD_DIM = 128
N_HEADS = D_ATTN // HEAD_DIM
ROT_DIM = HEAD_DIM // 4
ROPE_THETA = 500000.0
MOBA_BLOCK = 256
MOBA_TOP_K = 3
Q_CHUNK = 64
D_IN = D_SSM + 3 * D_ATTN + 2 * D_MODEL

kernel_name = "hybrid_s5_moba_macaron_gated"


def rms_norm(x, g):
    x32 = x.astype(jnp.float32)
    y = x32 * lax.rsqrt(jnp.mean(x32 * x32, axis=-1, keepdims=True) + RMS_EPS)
    return (y * g.astype(jnp.float32)).astype(x.dtype)


def swiglu(x, w_gate, w_up, w_down):
    return (jax.nn.silu(x @ w_gate) * (x @ w_up)) @ w_down


def rotary_tables(seq_len):
    pos = jnp.arange(seq_len, dtype=jnp.float32)
    inv_freq = 1.0 / (ROPE_THETA ** (jnp.arange(0, ROT_DIM, 2, dtype=jnp.float32) / ROT_DIM))
    ang = pos[:, None] * inv_freq[None, :]
    return jnp.cos(ang), jnp.sin(ang)


def apply_partial_rotary(x, cos, sin):
    half = ROT_DIM // 2
    cos = cos.astype(x.dtype)
    sin = sin.astype(x.dtype)
    x1 = x[..., :half]
    x2 = x[..., half:ROT_DIM]
    return jnp.concatenate([x1 * cos - x2 * sin, x2 * cos + x1 * sin, x[..., ROT_DIM:]], axis=-1)


def split_heads(t):
    bsz, seq_len, _ = t.shape
    return t.reshape(bsz, seq_len, N_HEADS, HEAD_DIM).transpose(0, 2, 1, 3)


def _complex_scan_combine(left, right):
    a1r, a1i, b1r, b1i = left
    a2r, a2i, b2r, b2i = right
    return (a1r * a2r - a1i * a2i,
            a1r * a2i + a1i * a2r,
            a2r * b1r - a2i * b1i + b2r,
            a2r * b1i + a2i * b1r + b2i)


def s5_ssm(u, a_re, a_im, log_dt, b_re, b_im, c_re, c_im, d_skip):
    f32 = jnp.float32
    bsz, seq_len, _ = u.shape
    u32 = u.astype(f32).reshape(bsz, seq_len, SSM_GROUPS, SSM_GROUP)
    a_re = a_re.astype(f32)
    a_im = a_im.astype(f32)
    dt = jnp.exp(log_dt.astype(f32))[:, None]
    mag = jnp.exp(a_re * dt)
    lam_re = mag * jnp.cos(a_im * dt)
    lam_im = mag * jnp.sin(a_im * dt)
    den = a_re * a_re + a_im * a_im
    num_re = lam_re - 1.0
    coef_re = (num_re * a_re + lam_im * a_im) / den
    coef_im = (lam_im * a_re - num_re * a_im) / den
    bu_re = jnp.einsum('blgp,gnp->blgn', u32, b_re.astype(f32))
    bu_im = jnp.einsum('blgp,gnp->blgn', u32, b_im.astype(f32))
    in_re = coef_re * bu_re - coef_im * bu_im
    in_im = coef_re * bu_im + coef_im * bu_re
    lam_re_t = jnp.broadcast_to(lam_re, in_re.shape)
    lam_im_t = jnp.broadcast_to(lam_im, in_im.shape)
    _, _, s_re, s_im = lax.associative_scan(
        _complex_scan_combine, (lam_re_t, lam_im_t, in_re, in_im), axis=1)
    y = (jnp.einsum('blgn,gpn->blgp', s_re, c_re.astype(f32))
         - jnp.einsum('blgn,gpn->blgp', s_im, c_im.astype(f32)))
    y = y.reshape(bsz, seq_len, D_SSM) + d_skip.astype(f32) * u32.reshape(bsz, seq_len, D_SSM)
    return y.astype(u.dtype)


def _gather_blocks(xb, ids):
    return jax.vmap(jax.vmap(lambda t, i: t[i]))(xb, ids)


def moba_attention(q, k, v):
    f32 = jnp.float32
    bsz, n_heads, seq_len, hd = q.shape
    nb = -(-seq_len // MOBA_BLOCK)
    pad = nb * MOBA_BLOCK - seq_len
    kb = jnp.pad(k, ((0, 0), (0, 0), (0, pad), (0, 0))).reshape(bsz, n_heads, nb, MOBA_BLOCK, hd)
    vb = jnp.pad(v, ((0, 0), (0, 0), (0, pad), (0, 0))).reshape(bsz, n_heads, nb, MOBA_BLOCK, hd)
    scale = 1.0 / math.sqrt(hd)
    k_mean = jnp.mean(kb.astype(f32), axis=3)
    gate = jnp.einsum('bhtd,bhnd->bhtn', q.astype(f32), k_mean)
    q_blk = jnp.arange(seq_len) // MOBA_BLOCK
    past = jnp.arange(nb)[None, :] < q_blk[:, None]
    gate = jnp.where(past, gate, -jnp.inf)
    n_sel = min(MOBA_TOP_K, nb)
    _, sel = lax.top_k(gate, n_sel)
    sel_valid = jnp.arange(n_sel)[None, :] < q_blk[:, None]

    def one_chunk(c):
        start = c * Q_CHUNK
        qc = lax.dynamic_slice_in_dim(q, start, Q_CHUNK, axis=2)
        ids = lax.dynamic_slice_in_dim(sel, start, Q_CHUNK, axis=2)
        ok = lax.dynamic_slice_in_dim(sel_valid, start, Q_CHUNK, axis=0)
        own = start // MOBA_BLOCK
        k_own = lax.dynamic_index_in_dim(kb, own, axis=2, keepdims=False)
        v_own = lax.dynamic_index_in_dim(vb, own, axis=2, keepdims=False)
        q_pos = start + jnp.arange(Q_CHUNK)
        k_pos = own * MOBA_BLOCK + jnp.arange(MOBA_BLOCK)
        s_own = jnp.einsum('bhqd,bhkd->bhqk', qc, k_own).astype(f32) * scale
        scores = [jnp.where(k_pos[None, :] <= q_pos[:, None], s_own, -jnp.inf)]
        for slot in range(n_sel):
            kg = _gather_blocks(kb, ids[..., slot])
            s = jnp.einsum('bhqd,bhqkd->bhqk', qc, kg).astype(f32) * scale
            scores.append(jnp.where(ok[:, slot, None], s, -jnp.inf))
        probs = jax.nn.softmax(jnp.concatenate(scores, axis=-1), axis=-1).astype(v.dtype)
        out = jnp.einsum('bhqk,bhkd->bhqd', probs[..., :MOBA_BLOCK], v_own)
        for slot in range(n_sel):
            vg = _gather_blocks(vb, ids[..., slot])
            lo = (slot + 1) * MOBA_BLOCK
            out = out + jnp.einsum('bhqk,bhqkd->bhqd', probs[..., lo:lo + MOBA_BLOCK], vg)
        return out

    outs = lax.map(one_chunk, jnp.arange(seq_len // Q_CHUNK))
    return outs.transpose(1, 2, 0, 3, 4).reshape(bsz, n_heads, seq_len, hd)


def setup_inputs(seed: int = 0) -> dict:
    key = jax.random.key(seed)
    ks = jax.random.split(key, 32)
    f32 = jnp.float32

    def lin(k, shape):
        return jax.random.normal(k, shape, f32) * (shape[-2] ** -0.5)

    def gain(k, shape):
        return 1.0 + 0.02 * jax.random.normal(k, shape, f32)

    n_idx = jnp.arange(SSM_STATE, dtype=f32)
    return {
        "x": jax.random.normal(ks[0], (BATCH, SEQ, D_MODEL), f32),
        "p": jax.random.normal(ks[1], (DEPTH, BATCH, SEQ, PLE_DIM), f32),
        "ffn1_norm": gain(ks[2], (DEPTH, D_MODEL)),
        "ffn1_w_gate": lin(ks[3], (DEPTH, D_MODEL, D_FF)),
        "ffn1_w_up": lin(ks[4], (DEPTH, D_MODEL, D_FF)),
        "ffn1_w_down": lin(ks[5], (DEPTH, D_FF, D_MODEL)),
        "mix_norm": gain(ks[6], (DEPTH, D_MODEL)),
        "w_in": lin(ks[7], (DEPTH, D_MODEL, D_IN)),
        "ssm_a_re": -0.5 + 0.01 * jax.random.normal(ks[8], (DEPTH, SSM_GROUPS, SSM_STATE), f32),
        "ssm_a_im": math.pi * n_idx + 0.01 * jax.random.normal(ks[9], (DEPTH, SSM_GROUPS, SSM_STATE), f32),
        "ssm_log_dt": jax.random.uniform(ks[10], (DEPTH, SSM_GROUPS), f32,
                                         minval=math.log(1e-3), maxval=math.log(1e-1)),
        "ssm_b_re": jax.random.normal(ks[11], (DEPTH, SSM_GROUPS, SSM_STATE, SSM_GROUP), f32) * (2 * SSM_GROUP) ** -0.5,
        "ssm_b_im": jax.random.normal(ks[12], (DEPTH, SSM_GROUPS, SSM_STATE, SSM_GROUP), f32) * (2 * SSM_GROUP) ** -0.5,
        "ssm_c_re": jax.random.normal(ks[13], (DEPTH, SSM_GROUPS, SSM_GROUP, SSM_STATE), f32) * (2 * SSM_STATE) ** -0.5,
        "ssm_c_im": jax.random.normal(ks[14], (DEPTH, SSM_GROUPS, SSM_GROUP, SSM_STATE), f32) * (2 * SSM_STATE) ** -0.5,
        "ssm_d": jax.random.normal(ks[15], (DEPTH, D_SSM), f32),
        "ssm_w_glu": lin(ks[16], (DEPTH, D_SSM, 2 * D_SSM)),
        "w_branch_ssm": lin(ks[17], (DEPTH, D_SSM, D_MODEL)),
        "w_branch_attn": lin(ks[18], (DEPTH, D_ATTN, D_MODEL)),
        "w_out": lin(ks[19], (DEPTH, D_MODEL, D_MODEL)),
        "ffn2_norm": gain(ks[20], (DEPTH, D_MODEL)),
        "ffn2_w_gate": lin(ks[21], (DEPTH, D_MODEL, D_FF)),
        "ffn2_w_up": lin(ks[22], (DEPTH, D_MODEL, D_FF)),
        "ffn2_w_down": lin(ks[23], (DEPTH, D_FF, D_MODEL)),
        "ple_norm": gain(ks[24], (DEPTH, D_MODEL)),
        "ple_w_up": lin(ks[25], (DEPTH, PLE_DIM, D_MODEL)),
        "ple_w_gate": lin(ks[26], (DEPTH, D_MODEL, D_MODEL)),
        "final_norm": gain(ks[27], (D_MODEL,)),
    }


def reference(x, p, ffn1_norm, ffn1_w_gate, ffn1_w_up, ffn1_w_down, mix_norm, w_in,
              ssm_a_re, ssm_a_im, ssm_log_dt, ssm_b_re, ssm_b_im, ssm_c_re, ssm_c_im, ssm_d,
              ssm_w_glu, w_branch_ssm, w_branch_attn, w_out, ffn2_norm, ffn2_w_gate, ffn2_w_up,
              ffn2_w_down, ple_norm, ple_w_up, ple_w_gate, final_norm):
    bsz, seq_len, _ = x.shape
    cos, sin = rotary_tables(seq_len)
    cuts = [D_SSM, D_SSM + D_ATTN, D_SSM + 2 * D_ATTN, D_SSM + 3 * D_ATTN,
            D_SSM + 3 * D_ATTN + D_MODEL]
    h = x
    for i in range(DEPTH):
        h = h + 0.5 * swiglu(rms_norm(h, ffn1_norm[i]), ffn1_w_gate[i], ffn1_w_up[i], ffn1_w_down[i])
        u = rms_norm(h, mix_norm[i])
        z = u @ w_in[i]
        z_ssm, q, k, v, g_ssm, g_attn = jnp.split(z, cuts, axis=-1)
        y_a = jax.nn.gelu(s5_ssm(z_ssm, ssm_a_re[i], ssm_a_im[i], ssm_log_dt[i], ssm_b_re[i],
                                 ssm_b_im[i], ssm_c_re[i], ssm_c_im[i], ssm_d[i]))
        glu = y_a @ ssm_w_glu[i]
        y_a = glu[..., :D_SSM] * jax.nn.sigmoid(glu[..., D_SSM:])
        qh = apply_partial_rotary(split_heads(q), cos, sin)
        kh = apply_partial_rotary(split_heads(k), cos, sin)
        vh = split_heads(v)
        y_b = moba_attention(qh, kh, vh).transpose(0, 2, 1, 3).reshape(bsz, seq_len, D_ATTN)
        merged = (jax.nn.sigmoid(g_ssm) * (y_a @ w_branch_ssm[i])
                  + jax.nn.sigmoid(g_attn) * (y_b @ w_branch_attn[i]))
        h = h + merged @ w_out[i]
        h = h + 0.5 * swiglu(rms_norm(h, ffn2_norm[i]), ffn2_w_gate[i], ffn2_w_up[i], ffn2_w_down[i])
        h = h + (p[i] @ ple_w_up[i]) * jax.nn.sigmoid(rms_norm(h, ple_norm[i]) @ ple_w_gate[i])
    return rms_norm(h, final_norm)
```

```python
import functools
import math

import jax
import jax.numpy as jnp
from jax import lax
from jax.experimental import pallas as pl
from jax.experimental.pallas import tpu as pltpu

F32 = jnp.float32
BF16 = jnp.bfloat16

RMS_EPS = 1e-6
SSM_GROUP = 16
SSM_STATE = 64
HEAD_DIM = 128
ROT_DIM = HEAD_DIM // 4
ROPE_THETA = 500000.0
MOBA_BLOCK = 256
MOBA_TOP_K = 3

LANES = 128
SUBLANES = 8
VMEM_LIMIT = 56 * 1024 * 1024

SSM_BATCH = 4
SSM_CHUNK = 128
SSM_PITCH = SSM_CHUNK + 4


def _rms(x, g):
    ms = jnp.mean(x * x, axis=-1, keepdims=True)
    return x * lax.rsqrt(ms + RMS_EPS) * g


def _dot(a, b):
    return jnp.dot(a, b, preferred_element_type=F32)


def _dot_nt(a, b):
    return lax.dot_general(a, b, (((1,), (1,)), ((), ())), preferred_element_type=F32)


def _params(*sem):
    return pltpu.CompilerParams(dimension_semantics=sem, vmem_limit_bytes=VMEM_LIMIT)


def _resident(shape):
    nd = len(shape)
    return pl.BlockSpec(shape, lambda *_: (0,) * nd, pipeline_mode=pl.Buffered(1))


def _ffn_kernel(h_ref, g_ref, wg_ref, wu_ref, wd_ref, o_ref, xn_ref):
    j = pl.program_id(1)

    @pl.when(j == 0)
    def _():
        h = h_ref[...]
        xn_ref[...] = _rms(h, g_ref[...]).astype(BF16)
        o_ref[...] = h

    xn = xn_ref[...]
    gate = _dot(xn, wg_ref[...])
    up = _dot(xn, wu_ref[...])
    act = (gate * jax.nn.sigmoid(gate) * up).astype(BF16)
    o_ref[...] += 0.5 * _dot(act, wd_ref[...])


def _ffn(h, g, wg, wu, wd, *, tm=512, tf=512):
    t, d = h.shape
    f = wg.shape[1]
    assert t % tm == 0 and f % tf == 0
    return pl.pallas_call(
        _ffn_kernel,
        grid=(t // tm, f // tf),
        in_specs=[
            pl.BlockSpec((tm, d), lambda i, j: (i, 0)),
            pl.BlockSpec((1, d), lambda i, j: (0, 0)),
            pl.BlockSpec((d, tf), lambda i, j: (0, j)),
            pl.BlockSpec((d, tf), lambda i, j: (0, j)),
            pl.BlockSpec((tf, d), lambda i, j: (j, 0)),
        ],
        out_specs=pl.BlockSpec((tm, d), lambda i, j: (i, 0)),
        out_shape=jax.ShapeDtypeStruct((t, d), F32),
        scratch_shapes=[pltpu.VMEM((tm, d), BF16)],
        compiler_params=_params("parallel", "arbitrary"),
        name="ffn",
    )(h, g, wg, wu, wd)


def _in_proj_kernel(h_ref, g_ref, w_ref, o_ref, xn_ref):
    @pl.when(pl.program_id(1) == 0)
    def _():
        xn_ref[...] = _rms(h_ref[...], g_ref[...]).astype(BF16)

    o_ref[...] = _dot(xn_ref[...], w_ref[...])


def _in_proj(h, g, w, *, tm=512, tn=1024):
    t, d = h.shape
    n = w.shape[1]
    assert t % tm == 0 and n % tn == 0
    return pl.pallas_call(
        _in_proj_kernel,
        grid=(t // tm, n // tn),
        in_specs=[
            pl.BlockSpec((tm, d), lambda i, j: (i, 0)),
            pl.BlockSpec((1, d), lambda i, j: (0, 0)),
            pl.BlockSpec((d, tn), lambda i, j: (0, j)),
        ],
        out_specs=pl.BlockSpec((tm, tn), lambda i, j: (i, j)),
        out_shape=jax.ShapeDtypeStruct((t, n), F32),
        scratch_shapes=[pltpu.VMEM((tm, d), BF16)],
        compiler_params=_params("parallel", "arbitrary"),
        name="in_proj",
    )(h, g, w)


def _gelu_tanh(x):
    c = math.sqrt(2.0 / math.pi)
    return 0.5 * x * (1.0 + jnp.tanh(c * (x + 0.044715 * (x * x * x))))


def _ssm_kernel(z_ref, wre_ref, wim_ref, wc_ref, lr_ref, li_ref, d_ref, o_ref, bu_ref, st_ref):
    tc = z_ref.shape[1]
    pitch = SSM_PITCH
    n_slabs = bu_ref.shape[0]
    slabs_per_kt = wre_ref.shape[2] // LANES
    n_kt = wre_ref.shape[0]

    @pl.when(pl.program_id(0) == 0)
    def _():
        st_ref[...] = jnp.zeros_like(st_ref)

    for b in range(SSM_BATCH):
        for kt in range(n_kt):
            u = z_ref[b, :, kt * LANES:(kt + 1) * LANES].astype(BF16)
            r_re = _dot(u, wre_ref[kt])
            r_im = _dot(u, wim_ref[kt])
            for q in range(slabs_per_kt):
                s = kt * slabs_per_kt + q
                bu_ref[s, pl.ds(b * pitch, tc), :] = r_re[:, q * LANES:(q + 1) * LANES]
                bu_ref[s, pl.ds((SSM_BATCH + b) * pitch, tc), :] = r_im[:, q * LANES:(q + 1) * LANES]

    group = 8
    for g0 in range(0, n_slabs, group):
        slabs = list(range(g0, g0 + group))
        lrs = [lr_ref[s] for s in slabs]
        lis = [li_ref[s] for s in slabs]

        def body(t, carry, slabs=slabs, lrs=lrs, lis=lis):
            new = []
            for idx, s in enumerate(slabs):
                v = bu_ref[s, pl.ds(t, SUBLANES, stride=pitch), :]
                st = carry[idx]
                st = st * lrs[idx] + pltpu.roll(st, SSM_BATCH, axis=0) * lis[idx] + v
                bu_ref[s, pl.ds(t, SUBLANES, stride=pitch), :] = st
                new.append(st)
            return tuple(new)

        fin = lax.fori_loop(0, tc, body, tuple(st_ref[s] for s in slabs), unroll=4)
        for idx, s in enumerate(slabs):
            st_ref[s] = fin[idx]

    n_nt = wc_ref.shape[0]
    slabs_per_nt = n_slabs // n_nt
    for b in range(SSM_BATCH):
        for nt in range(n_nt):
            parts = [bu_ref[nt * slabs_per_nt + q, pl.ds(b * pitch, tc), :] for q in range(slabs_per_nt)]
            parts += [bu_ref[nt * slabs_per_nt + q, pl.ds((SSM_BATCH + b) * pitch, tc), :]
                      for q in range(slabs_per_nt)]
            lhs = jnp.concatenate(parts, axis=1).astype(BF16)
            y = _dot(lhs, wc_ref[nt])
            cols = slice(nt * LANES, (nt + 1) * LANES)
            y = y + d_ref[:, cols] * z_ref[b, :, cols]
            o_ref[b, :, cols] = _gelu_tanh(y)


def _ssm_weights(a_re, a_im, log_dt, b_re, b_im, c_re, c_im):
    g, n = a_re.shape
    p = b_re.shape[-1]
    dt = jnp.exp(log_dt)[:, None]
    mag = jnp.exp(a_re * dt)
    lam_re = mag * jnp.cos(a_im * dt)
    lam_im = mag * jnp.sin(a_im * dt)
    den = a_re * a_re + a_im * a_im
    num_re = lam_re - 1.0
    coef_re = (num_re * a_re + lam_im * a_im) / den
    coef_im = (lam_im * a_re - num_re * a_im) / den
    bb_re = coef_re[..., None] * b_re - coef_im[..., None] * b_im
    bb_im = coef_re[..., None] * b_im + coef_im[..., None] * b_re

    gpt = LANES // p
    n_t = g // gpt
    eye = jnp.eye(gpt, dtype=F32)

    def in_tiles(bb):
        bb = bb.reshape(n_t, gpt, n, p)
        w = jnp.einsum("ab,tanp->tapbn", eye, bb)
        return w.reshape(n_t, gpt * p, gpt * n).astype(BF16)

    def out_tiles(c):
        c = c.reshape(n_t, gpt, p, n)
        w = jnp.einsum("ab,tapn->tanbp", eye, c)
        return w.reshape(n_t, gpt * n, gpt * p)

    wc = jnp.concatenate([out_tiles(c_re), -out_tiles(c_im)], axis=1).astype(BF16)
    n_slabs = g * n // LANES
    sign = jnp.concatenate([-jnp.ones((SSM_BATCH, 1), F32), jnp.ones((SSM_BATCH, 1), F32)], axis=0)
    lr = jnp.broadcast_to(lam_re.reshape(n_slabs, 1, LANES), (n_slabs, SUBLANES, LANES))
    li = lam_im.reshape(n_slabs, 1, LANES) * sign[None]
    return in_tiles(bb_re), in_tiles(bb_im), wc, lr, li


def _ssm(z3, ssm_w, d_skip):
    bsz, seq, _ = z3.shape
    wre, wim, wc, lr, li = ssm_w
    d_ssm = d_skip.shape[-1]
    n_slabs = lr.shape[0]
    tc = SSM_CHUNK
    assert bsz == SSM_BATCH and seq % tc == 0 and 2 * SSM_BATCH == SUBLANES
    return pl.pallas_call(
        _ssm_kernel,
        grid=(seq // tc,),
        in_specs=[
            pl.BlockSpec((bsz, tc, d_ssm), lambda c: (0, c, 0)),
            _resident(wre.shape), _resident(wim.shape), _resident(wc.shape),
            _resident(lr.shape), _resident(li.shape), _resident(d_skip.shape),
        ],
        out_specs=pl.BlockSpec((bsz, tc, d_ssm), lambda c: (0, c, 0)),
        out_shape=jax.ShapeDtypeStruct((bsz, seq, d_ssm), F32),
        scratch_shapes=[
            pltpu.VMEM((n_slabs, SUBLANES * SSM_PITCH, LANES), F32),
            pltpu.VMEM((n_slabs, SUBLANES, LANES), F32),
        ],
        compiler_params=_params("arbitrary"),
        name="ssm",
    )(z3, wre, wim, wc, lr, li, d_skip)


def _rotary_tables(seq_len):
    pos = jnp.arange(seq_len, dtype=F32)
    inv_freq = 1.0 / (ROPE_THETA ** (jnp.arange(0, ROT_DIM, 2, dtype=F32) / ROT_DIM))
    ang = pos[:, None] * inv_freq[None, :]
    cos, sin = jnp.cos(ang), jnp.sin(ang)
    rest = HEAD_DIM - ROT_DIM
    cos_t = jnp.concatenate([cos, cos, jnp.ones((seq_len, rest), F32)], axis=1)
    sin_t = jnp.concatenate([-sin, sin, jnp.zeros((seq_len, rest), F32)], axis=1)
    return cos_t, sin_t


def _split_bf16(x):
    hi = x.astype(BF16)
    lo = (x - hi.astype(F32)).astype(BF16)
    return hi, lo


def _moba_kernel(q_ref, k_ref, v_ref, cos_ref, sin_ref, o_ref, qb_ref, kb_ref, vt_ref, km_ref):
    seq = q_ref.shape[0]
    blk = MOBA_BLOCK
    nb = seq // blk
    half = ROT_DIM // 2
    scale = 1.0 / math.sqrt(HEAD_DIM)
    lane = lax.broadcasted_iota(jnp.int32, (blk, HEAD_DIM), 1)

    def rot(x, rows):
        swapped = jnp.where(lane < half,
                            pltpu.roll(x, HEAD_DIM - half, axis=1),
                            pltpu.roll(x, half, axis=1))
        return x * cos_ref[rows, :] + swapped * sin_ref[rows, :]

    for j in range(nb):
        rows = slice(j * blk, (j + 1) * blk)
        qb_ref[rows, :] = rot(q_ref[rows, :], rows)
        kr = rot(k_ref[rows, :], rows)
        kb_ref[rows, :] = kr.astype(BF16)
        km_ref[j:j + 1, :] = jnp.mean(kr, axis=0, keepdims=True)
        vt_ref[:, rows] = v_ref[rows, :].T.astype(BF16)

    km_hi, km_lo = _split_bf16(km_ref[...])
    blk_id = lax.broadcasted_iota(jnp.int32, (nb, blk), 0)
    key_pos = lax.broadcasted_iota(jnp.int32, (blk, blk), 0)
    qry_pos = lax.broadcasted_iota(jnp.int32, (blk, blk), 1)
    neg_inf = jnp.float32(-jnp.inf)

    for i in range(nb):
        rows = slice(i * blk, (i + 1) * blk)
        q = qb_ref[rows, :]
        q_hi, q_lo = _split_bf16(q)
        n_keys = (i + 1) * blk
        s = _dot_nt(kb_ref[0:n_keys, :], q_hi) * scale
        tiles = []
        if i > MOBA_TOP_K:
            gate = _dot_nt(km_hi, q_hi) + _dot_nt(km_hi, q_lo) + _dot_nt(km_lo, q_hi)
            past = blk_id < i
        for j in range(i):
            sj = s[j * blk:(j + 1) * blk, :]
            if i > MOBA_TOP_K:
                gj = gate[j:j + 1, :]
                beats = ((gate > gj) | ((gate == gj) & (blk_id < j))) & past
                cnt = jnp.sum(beats.astype(F32), axis=0, keepdims=True)
                sj = jnp.where(cnt < MOBA_TOP_K, sj, neg_inf)
            tiles.append(sj)
        tiles.append(jnp.where(key_pos <= qry_pos, s[i * blk:n_keys, :], neg_inf))
        m = tiles[0]
        for tl in tiles[1:]:
            m = jnp.maximum(m, tl)
        m = jnp.max(m, axis=0, keepdims=True)
        probs = [jnp.exp(tl - m) for tl in tiles]
        den = probs[0]
        for pr in probs[1:]:
            den = den + pr
        den = jnp.sum(den, axis=0, keepdims=True)
        p_all = jnp.concatenate([pr.astype(BF16) for pr in probs], axis=0)
        out_t = _dot(vt_ref[:, 0:n_keys], p_all) / den
        o_ref[rows, :] = out_t.T


def _moba(z3, cos_t, sin_t, *, d_ssm, d_attn):
    bsz, seq, _ = z3.shape
    n_heads = d_attn // HEAD_DIM
    assert seq % MOBA_BLOCK == 0
    q0 = d_ssm // HEAD_DIM
    k0 = q0 + n_heads
    v0 = k0 + n_heads
    head_block = lambda c0: pl.BlockSpec((None, seq, HEAD_DIM), lambda b, h: (b, 0, c0 + h))
    return pl.pallas_call(
        _moba_kernel,
        grid=(bsz, n_heads),
        in_specs=[head_block(q0), head_block(k0), head_block(v0),
                  _resident(cos_t.shape), _resident(sin_t.shape)],
        out_specs=pl.BlockSpec((None, seq, HEAD_DIM), lambda b, h: (b, 0, h)),
        out_shape=jax.ShapeDtypeStruct((bsz, seq, d_attn), F32),
        scratch_shapes=[
            pltpu.VMEM((seq, HEAD_DIM), F32),
            pltpu.VMEM((seq, HEAD_DIM), BF16),
            pltpu.VMEM((HEAD_DIM, seq), BF16),
            pltpu.VMEM((seq // MOBA_BLOCK, HEAD_DIM), F32),
        ],
        compiler_params=_params("parallel", "parallel"),
        name="moba",
    )(z3, z3, z3, cos_t, sin_t)


def _merge_kernel(ya_ref, yb_ref, gs_ref, ga_ref, h_ref, wglu_ref, wa_ref, wb_ref, wo_ref, o_ref):
    d_ssm = ya_ref.shape[1]
    glu = _dot(ya_ref[...].astype(BF16), wglu_ref[...])
    a = (glu[:, :d_ssm] * jax.nn.sigmoid(glu[:, d_ssm:])).astype(BF16)
    merged = (jax.nn.sigmoid(gs_ref[...]) * _dot(a, wa_ref[...])
              + jax.nn.sigmoid(ga_ref[...]) * _dot(yb_ref[...].astype(BF16), wb_ref[...]))
    o_ref[...] = h_ref[...] + _dot(merged.astype(BF16), wo_ref[...])


def _merge(ya, yb, z, h, wglu, wa, wb, wo, *, tm=256):
    t, d = h.shape
    d_ssm = ya.shape[1]
    d_attn = yb.shape[1]
    gs_blk = (d_ssm + 3 * d_attn) // d
    assert (d_ssm + 3 * d_attn) % d == 0 and t % tm == 0
    return pl.pallas_call(
        _merge_kernel,
        grid=(t // tm,),
        in_specs=[
            pl.BlockSpec((tm, d_ssm), lambda i: (i, 0)),
            pl.BlockSpec((tm, d_attn), lambda i: (i, 0)),
            pl.BlockSpec((tm, d), lambda i: (i, gs_blk)),
            pl.BlockSpec((tm, d), lambda i: (i, gs_blk + 1)),
            pl.BlockSpec((tm, d), lambda i: (i, 0)),
            _resident(wglu.shape), _resident(wa.shape), _resident(wb.shape), _resident(wo.shape),
        ],
        out_specs=pl.BlockSpec((tm, d), lambda i: (i, 0)),
        out_shape=jax.ShapeDtypeStruct((t, d), F32),
        compiler_params=_params("parallel"),
        name="merge",
    )(ya, yb, z, z, h, wglu, wa, wb, wo)


def _ple_kernel(h_ref, p_ref, g_ref, wup_ref, wgate_ref, fg_ref, o_ref, *, final):
    h = h_ref[...]
    gate = jax.nn.sigmoid(_dot(_rms(h, g_ref[...]).astype(BF16), wgate_ref[...]))
    h = h + _dot(p_ref[...].astype(BF16), wup_ref[...]) * gate
    if final:
        h = _rms(h, fg_ref[...])
    o_ref[...] = h


def _ple(h, p, g, wup, wgate, fg, *, final, tm=512):
    t, d = h.shape
    dp = p.shape[1]
    assert t % tm == 0
    return pl.pallas_call(
        functools.partial(_ple_kernel, final=final),
        grid=(t // tm,),
        in_specs=[
            pl.BlockSpec((tm, d), lambda i: (i, 0)),
            pl.BlockSpec((tm, dp), lambda i: (i, 0)),
            pl.BlockSpec((1, d), lambda i: (0, 0)),
            _resident(wup.shape), _resident(wgate.shape),
            pl.BlockSpec((1, d), lambda i: (0, 0)),
        ],
        out_specs=pl.BlockSpec((tm, d), lambda i: (i, 0)),
        out_shape=jax.ShapeDtypeStruct((t, d), F32),
        compiler_params=_params("parallel"),
        name="ple",
    )(h, p, g, wup, wgate, fg)


def kernel(x, p, ffn1_norm, ffn1_w_gate, ffn1_w_up, ffn1_w_down, mix_norm, w_in, ssm_a_re, ssm_a_im, ssm_log_dt, ssm_b_re, ssm_b_im, ssm_c_re, ssm_c_im, ssm_d, ssm_w_glu, w_branch_ssm, w_branch_attn, w_out, ffn2_norm, ffn2_w_gate, ffn2_w_up, ffn2_w_down, ple_norm, ple_w_up, ple_w_gate, final_norm):
    bsz, seq, d = x.shape
    depth = p.shape[0]
    t = bsz * seq
    d_ssm = ssm_d.shape[-1]
    d_attn = w_branch_attn.shape[-2]
    d_in = w_in.shape[-1]
    cos_t, sin_t = _rotary_tables(seq)
    bf = lambda w: w.astype(BF16)
    row = lambda v: v.reshape(1, -1)

    h = x.reshape(t, d)
    for i in range(depth):
        h = _ffn(h, row(ffn1_norm[i]), bf(ffn1_w_gate[i]), bf(ffn1_w_up[i]), bf(ffn1_w_down[i]))
        z = _in_proj(h, row(mix_norm[i]), bf(w_in[i]))
        z3 = z.reshape(bsz, seq, d_in)
        ssm_w = _ssm_weights(ssm_a_re[i], ssm_a_im[i], ssm_log_dt[i], ssm_b_re[i], ssm_b_im[i],
                             ssm_c_re[i], ssm_c_im[i])
        ya = _ssm(z3, ssm_w, row(ssm_d[i]))
        yb = _moba(z3, cos_t, sin_t, d_ssm=d_ssm, d_attn=d_attn)
        h = _merge(ya.reshape(t, d_ssm), yb.reshape(t, d_attn), z, h,
                   bf(ssm_w_glu[i]), bf(w_branch_ssm[i]), bf(w_branch_attn[i]), bf(w_out[i]))
        h = _ffn(h, row(ffn2_norm[i]), bf(ffn2_w_gate[i]), bf(ffn2_w_up[i]), bf(ffn2_w_down[i]))
        h = _ple(h, p[i].reshape(t, -1), row(ple_norm[i]), bf(ple_w_up[i]), bf(ple_w_gate[i]),
                 row(final_norm), final=(i == depth - 1))
    return h.reshape(bsz, seq, d)
```

```python
import functools
import math

import jax
import jax.numpy as jnp
from jax import lax
from jax.experimental import pallas as pl
from jax.experimental.pallas import tpu as pltpu

F32 = jnp.float32
BF16 = jnp.bfloat16

RMS_EPS = 1e-6
SSM_GROUP = 16
SSM_STATE = 64
HEAD_DIM = 128
ROT_DIM = HEAD_DIM // 4
ROPE_THETA = 500000.0
MOBA_BLOCK = 256
MOBA_TOP_K = 3

LANES = 128
SUBLANES = 8
VMEM_LIMIT = 56 * 1024 * 1024

SSM_BATCH = 4
SSM_CHUNK = 128
SSM_PITCH = SSM_CHUNK + 4


def _rms(x, g):
    ms = jnp.mean(x * x, axis=-1, keepdims=True)
    return x * lax.rsqrt(ms + RMS_EPS) * g


def _dot(a, b):
    return jnp.dot(a, b, preferred_element_type=F32)


def _dot_nt(a, b):
    return lax.dot_general(a, b, (((1,), (1,)), ((), ())), preferred_element_type=F32)


def _params(*sem):
    return pltpu.CompilerParams(dimension_semantics=sem, vmem_limit_bytes=VMEM_LIMIT)


def _resident(shape):
    nd = len(shape)
    return pl.BlockSpec(shape, lambda *_: (0,) * nd, pipeline_mode=pl.Buffered(1))


def _ffn_kernel(h_ref, g_ref, wg_ref, wu_ref, wd_ref, o_ref, xn_ref):
    j = pl.program_id(1)

    @pl.when(j == 0)
    def _():
        h = h_ref[...]
        xn_ref[...] = _rms(h, g_ref[...]).astype(BF16)
        o_ref[...] = h

    xn = xn_ref[...]
    gate = _dot(xn, wg_ref[...].astype(BF16))
    up = _dot(xn, wu_ref[...].astype(BF16))
    act = (0.5 * gate * jax.nn.sigmoid(gate) * up).astype(BF16)
    o_ref[...] += _dot(act, wd_ref[...].astype(BF16))


def _ffn(h, g, wg, wu, wd, layer, *, tm=1024, tf=256):
    t, d = h.shape
    f = wg.shape[-1]
    assert t % tm == 0 and f % tf == 0
    return pl.pallas_call(
        _ffn_kernel,
        grid=(t // tm, f // tf),
        in_specs=[
            pl.BlockSpec((tm, d), lambda i, j: (i, 0), pipeline_mode=pl.Buffered(1)),
            pl.BlockSpec((None, 1, d), lambda i, j: (layer, 0, 0)),
            pl.BlockSpec((None, d, tf), lambda i, j: (layer, 0, j)),
            pl.BlockSpec((None, d, tf), lambda i, j: (layer, 0, j)),
            pl.BlockSpec((None, tf, d), lambda i, j: (layer, j, 0)),
        ],
        out_specs=pl.BlockSpec((tm, d), lambda i, j: (i, 0)),
        out_shape=jax.ShapeDtypeStruct((t, d), F32),
        scratch_shapes=[pltpu.VMEM((tm, d), BF16)],
        compiler_params=_params("parallel", "arbitrary"),
        name="ffn",
    )(h, g, wg, wu, wd)


def _in_proj_kernel(h_ref, g_ref, w_ref, o_ref, xn_ref):
    @pl.when(pl.program_id(1) == 0)
    def _():
        xn_ref[...] = _rms(h_ref[...], g_ref[...]).astype(BF16)

    o_ref[...] = _dot(xn_ref[...], w_ref[...].astype(BF16))


def _in_proj(h, g, w, layer, *, tm=1024, tn=512):
    t, d = h.shape
    n = w.shape[-1]
    assert t % tm == 0 and n % tn == 0
    return pl.pallas_call(
        _in_proj_kernel,
        grid=(t // tm, n // tn),
        in_specs=[
            pl.BlockSpec((tm, d), lambda i, j: (i, 0), pipeline_mode=pl.Buffered(1)),
            pl.BlockSpec((None, 1, d), lambda i, j: (layer, 0, 0)),
            pl.BlockSpec((None, d, tn), lambda i, j: (layer, 0, j)),
        ],
        out_specs=pl.BlockSpec((tm, tn), lambda i, j: (i, j)),
        out_shape=jax.ShapeDtypeStruct((t, n), F32),
        scratch_shapes=[pltpu.VMEM((tm, d), BF16)],
        compiler_params=_params("parallel", "arbitrary"),
        name="in_proj",
    )(h, g, w)


def _gelu_tanh(x):
    c = math.sqrt(2.0 / math.pi)
    return 0.5 * x * (1.0 + jnp.tanh(c * (x + 0.044715 * (x * x * x))))


def _ssm_kernel(z_ref, wre_ref, wim_ref, wc_ref, lr_ref, li_ref, d_ref, o_ref, bu_ref, st_ref):
    tc = z_ref.shape[1]
    pitch = SSM_PITCH
    n_slabs = bu_ref.shape[0]
    slabs_per_kt = wre_ref.shape[2] // LANES
    n_kt = wre_ref.shape[0]

    @pl.when(pl.program_id(0) == 0)
    def _():
        st_ref[...] = jnp.zeros_like(st_ref)

    for b in range(SSM_BATCH):
        for kt in range(n_kt):
            u = z_ref[b, :, kt * LANES:(kt + 1) * LANES].astype(BF16)
            r_re = _dot(u, wre_ref[kt])
            r_im = _dot(u, wim_ref[kt])
            for q in range(slabs_per_kt):
                s = kt * slabs_per_kt + q
                bu_ref[s, pl.ds(b * pitch, tc), :] = r_re[:, q * LANES:(q + 1) * LANES]
                bu_ref[s, pl.ds((SSM_BATCH + b) * pitch, tc), :] = r_im[:, q * LANES:(q + 1) * LANES]

    group = 8
    for g0 in range(0, n_slabs, group):
        slabs = list(range(g0, g0 + group))
        lrs = [lr_ref[s] for s in slabs]
        lis = [li_ref[s] for s in slabs]

        def body(t, carry, slabs=slabs, lrs=lrs, lis=lis):
            new = []
            for idx, s in enumerate(slabs):
                v = bu_ref[s, pl.ds(t, SUBLANES, stride=pitch), :]
                st = carry[idx]
                st = st * lrs[idx] + pltpu.roll(st, SSM_BATCH, axis=0) * lis[idx] + v
                bu_ref[s, pl.ds(t, SUBLANES, stride=pitch), :] = st
                new.append(st)
            return tuple(new)

        fin = lax.fori_loop(0, tc, body, tuple(st_ref[s] for s in slabs), unroll=4)
        for idx, s in enumerate(slabs):
            st_ref[s] = fin[idx]

    n_nt = wc_ref.shape[0]
    slabs_per_nt = n_slabs // n_nt
    for b in range(SSM_BATCH):
        for nt in range(n_nt):
            parts = [bu_ref[nt * slabs_per_nt + q, pl.ds(b * pitch, tc), :] for q in range(slabs_per_nt)]
            parts += [bu_ref[nt * slabs_per_nt + q, pl.ds((SSM_BATCH + b) * pitch, tc), :]
                      for q in range(slabs_per_nt)]
            lhs = jnp.concatenate(parts, axis=1).astype(BF16)
            y = _dot(lhs, wc_ref[nt])
            cols = slice(nt * LANES, (nt + 1) * LANES)
            y = y + d_ref[:, cols] * z_ref[b, :, cols]
            o_ref[b, :, cols] = _gelu_tanh(y)


def _ssm_weights(a_re, a_im, log_dt, b_re, b_im, c_re, c_im):
    g, n = a_re.shape
    p = b_re.shape[-1]
    dt = jnp.exp(log_dt)[:, None]
    mag = jnp.exp(a_re * dt)
    lam_re = mag * jnp.cos(a_im * dt)
    lam_im = mag * jnp.sin(a_im * dt)
    den = a_re * a_re + a_im * a_im
    num_re = lam_re - 1.0
    coef_re = (num_re * a_re + lam_im * a_im) / den
    coef_im = (lam_im * a_re - num_re * a_im) / den
    bb_re = coef_re[..., None] * b_re - coef_im[..., None] * b_im
    bb_im = coef_re[..., None] * b_im + coef_im[..., None] * b_re

    gpt = LANES // p
    n_t = g // gpt
    eye = jnp.eye(gpt, dtype=F32)

    def in_tiles(bb):
        bb = bb.reshape(n_t, gpt, n, p)
        w = jnp.einsum("ab,tanp->tapbn", eye, bb)
        return w.reshape(n_t, gpt * p, gpt * n).astype(BF16)

    def out_tiles(c):
        c = c.reshape(n_t, gpt, p, n)
        w = jnp.einsum("ab,tapn->tanbp", eye, c)
        return w.reshape(n_t, gpt * n, gpt * p)

    wc = jnp.concatenate([out_tiles(c_re), -out_tiles(c_im)], axis=1).astype(BF16)
    n_slabs = g * n // LANES
    sign = jnp.concatenate([-jnp.ones((SSM_BATCH, 1), F32), jnp.ones((SSM_BATCH, 1), F32)], axis=0)
    lr = jnp.broadcast_to(lam_re.reshape(n_slabs, 1, LANES), (n_slabs, SUBLANES, LANES))
    li = lam_im.reshape(n_slabs, 1, LANES) * sign[None]
    return in_tiles(bb_re), in_tiles(bb_im), wc, lr, li


def _ssm(z3, ssm_w, d_skip):
    bsz, seq, _ = z3.shape
    wre, wim, wc, lr, li = ssm_w
    d_ssm = d_skip.shape[-1]
    n_slabs = lr.shape[0]
    tc = SSM_CHUNK
    assert bsz == SSM_BATCH and seq % tc == 0 and 2 * SSM_BATCH == SUBLANES
    return pl.pallas_call(
        _ssm_kernel,
        grid=(seq // tc,),
        in_specs=[
            pl.BlockSpec((bsz, tc, d_ssm), lambda c: (0, c, 0)),
            _resident(wre.shape), _resident(wim.shape), _resident(wc.shape),
            _resident(lr.shape), _resident(li.shape), _resident(d_skip.shape),
        ],
        out_specs=pl.BlockSpec((bsz, tc, d_ssm), lambda c: (0, c, 0)),
        out_shape=jax.ShapeDtypeStruct((bsz, seq, d_ssm), F32),
        scratch_shapes=[
            pltpu.VMEM((n_slabs, SUBLANES * SSM_PITCH, LANES), F32),
            pltpu.VMEM((n_slabs, SUBLANES, LANES), F32),
        ],
        compiler_params=_params("arbitrary"),
        name="ssm",
    )(z3, wre, wim, wc, lr, li, d_skip)


def _rotary_tables(seq_len):
    pos = jnp.arange(seq_len, dtype=F32)
    inv_freq = 1.0 / (ROPE_THETA ** (jnp.arange(0, ROT_DIM, 2, dtype=F32) / ROT_DIM))
    ang = pos[:, None] * inv_freq[None, :]
    cos, sin = jnp.cos(ang), jnp.sin(ang)
    rest = HEAD_DIM - ROT_DIM
    cos_t = jnp.concatenate([cos, cos, jnp.ones((seq_len, rest), F32)], axis=1)
    sin_t = jnp.concatenate([-sin, sin, jnp.zeros((seq_len, rest), F32)], axis=1)
    return cos_t, sin_t


def _split_bf16(x):
    hi = x.astype(BF16)
    lo = (x - hi.astype(F32)).astype(BF16)
    return hi, lo


def _moba_kernel(q_ref, k_ref, v_ref, cos_ref, sin_ref, o_ref, qb_ref, kb_ref, vt_ref, km_ref):
    seq = q_ref.shape[0]
    blk = MOBA_BLOCK
    nb = seq // blk
    half = ROT_DIM // 2
    scale = 1.0 / math.sqrt(HEAD_DIM)
    lane = lax.broadcasted_iota(jnp.int32, (blk, HEAD_DIM), 1)

    def rot(x, rows):
        swapped = jnp.where(lane < half,
                            pltpu.roll(x, HEAD_DIM - half, axis=1),
                            pltpu.roll(x, half, axis=1))
        return x * cos_ref[rows, :] + swapped * sin_ref[rows, :]

    for j in range(nb):
        rows = slice(j * blk, (j + 1) * blk)
        qb_ref[rows, :] = rot(q_ref[rows, :], rows)
        kr = rot(k_ref[rows, :], rows)
        kb_ref[rows, :] = kr.astype(BF16)
        km_ref[j:j + 1, :] = jnp.mean(kr, axis=0, keepdims=True)
        vt_ref[:, rows] = v_ref[rows, :].T.astype(BF16)

    km_hi, km_lo = _split_bf16(km_ref[...])
    blk_id = lax.broadcasted_iota(jnp.int32, (nb, blk), 0)
    key_pos = lax.broadcasted_iota(jnp.int32, (blk, blk), 0)
    qry_pos = lax.broadcasted_iota(jnp.int32, (blk, blk), 1)
    neg_inf = jnp.float32(-jnp.inf)

    for i in range(nb):
        rows = slice(i * blk, (i + 1) * blk)
        q = qb_ref[rows, :]
        q_hi, q_lo = _split_bf16(q)
        n_keys = (i + 1) * blk
        s = _dot_nt(kb_ref[0:n_keys, :], q_hi) * scale
        tiles = []
        if i > MOBA_TOP_K:
            gate = _dot_nt(km_hi, q_hi) + _dot_nt(km_hi, q_lo) + _dot_nt(km_lo, q_hi)
            past = blk_id < i
        for j in range(i):
            sj = s[j * blk:(j + 1) * blk, :]
            if i > MOBA_TOP_K:
                gj = gate[j:j + 1, :]
                beats = ((gate > gj) | ((gate == gj) & (blk_id < j))) & past
                cnt = jnp.sum(beats.astype(F32), axis=0, keepdims=True)
                sj = jnp.where(cnt < MOBA_TOP_K, sj, neg_inf)
            tiles.append(sj)
        tiles.append(jnp.where(key_pos <= qry_pos, s[i * blk:n_keys, :], neg_inf))
        m = tiles[0]
        for tl in tiles[1:]:
            m = jnp.maximum(m, tl)
        m = jnp.max(m, axis=0, keepdims=True)
        probs = [jnp.exp(tl - m) for tl in tiles]
        den = probs[0]
        for pr in probs[1:]:
            den = den + pr
        den = jnp.sum(den, axis=0, keepdims=True)
        p_all = jnp.concatenate([pr.astype(BF16) for pr in probs], axis=0)
        out_t = _dot(vt_ref[:, 0:n_keys], p_all) / den
        o_ref[rows, :] = out_t.T


def _moba(z3, cos_t, sin_t, *, d_ssm, d_attn):
    bsz, seq, _ = z3.shape
    n_heads = d_attn // HEAD_DIM
    assert seq % MOBA_BLOCK == 0
    q0 = d_ssm // HEAD_DIM
    k0 = q0 + n_heads
    v0 = k0 + n_heads
    head_block = lambda c0: pl.BlockSpec((None, seq, HEAD_DIM), lambda b, h: (b, 0, c0 + h))
    return pl.pallas_call(
        _moba_kernel,
        grid=(bsz, n_heads),
        in_specs=[head_block(q0), head_block(k0), head_block(v0),
                  _resident(cos_t.shape), _resident(sin_t.shape)],
        out_specs=pl.BlockSpec((None, seq, HEAD_DIM), lambda b, h: (b, 0, h)),
        out_shape=jax.ShapeDtypeStruct((bsz, seq, d_attn), F32),
        scratch_shapes=[
            pltpu.VMEM((seq, HEAD_DIM), F32),
            pltpu.VMEM((seq, HEAD_DIM), BF16),
            pltpu.VMEM((HEAD_DIM, seq), BF16),
            pltpu.VMEM((seq // MOBA_BLOCK, HEAD_DIM), F32),
        ],
        compiler_params=_params("parallel", "parallel"),
        name="moba",
    )(z3, z3, z3, cos_t, sin_t)


def _merge_kernel(ya_ref, yb_ref, gs_ref, ga_ref, h_ref, wglu_ref, wa_ref, wb_ref, wo_ref, o_ref):
    d_ssm = ya_ref.shape[1]
    glu = _dot(ya_ref[...].astype(BF16), wglu_ref[...])
    a = (glu[:, :d_ssm] * jax.nn.sigmoid(glu[:, d_ssm:])).astype(BF16)
    merged = (jax.nn.sigmoid(gs_ref[...]) * _dot(a, wa_ref[...])
              + jax.nn.sigmoid(ga_ref[...]) * _dot(yb_ref[...].astype(BF16), wb_ref[...]))
    o_ref[...] = h_ref[...] + _dot(merged.astype(BF16), wo_ref[...])


def _merge(ya, yb, z, h, wglu, wa, wb, wo, *, tm=256):
    t, d = h.shape
    d_ssm = ya.shape[1]
    d_attn = yb.shape[1]
    gs_blk = (d_ssm + 3 * d_attn) // d
    assert (d_ssm + 3 * d_attn) % d == 0 and t % tm == 0
    return pl.pallas_call(
        _merge_kernel,
        grid=(t // tm,),
        in_specs=[
            pl.BlockSpec((tm, d_ssm), lambda i: (i, 0)),
            pl.BlockSpec((tm, d_attn), lambda i: (i, 0)),
            pl.BlockSpec((tm, d), lambda i: (i, gs_blk)),
            pl.BlockSpec((tm, d), lambda i: (i, gs_blk + 1)),
            pl.BlockSpec((tm, d), lambda i: (i, 0)),
            _resident(wglu.shape), _resident(wa.shape), _resident(wb.shape), _resident(wo.shape),
        ],
        out_specs=pl.BlockSpec((tm, d), lambda i: (i, 0)),
        out_shape=jax.ShapeDtypeStruct((t, d), F32),
        compiler_params=_params("parallel"),
        name="merge",
    )(ya, yb, z, z, h, wglu, wa, wb, wo)


def _ple_kernel(h_ref, p_ref, g_ref, wup_ref, wgate_ref, fg_ref, o_ref, *, final):
    h = h_ref[...]
    gate = jax.nn.sigmoid(_dot(_rms(h, g_ref[...]).astype(BF16), wgate_ref[...]))
    h = h + _dot(p_ref[...].astype(BF16), wup_ref[...]) * gate
    if final:
        h = _rms(h, fg_ref[...])
    o_ref[...] = h


def _ple(h, p, g, wup, wgate, fg, *, final, tm=512):
    t, d = h.shape
    dp = p.shape[1]
    assert t % tm == 0
    return pl.pallas_call(
        functools.partial(_ple_kernel, final=final),
        grid=(t // tm,),
        in_specs=[
            pl.BlockSpec((tm, d), lambda i: (i, 0)),
            pl.BlockSpec((tm, dp), lambda i: (i, 0)),
            pl.BlockSpec((1, d), lambda i: (0, 0)),
            _resident(wup.shape), _resident(wgate.shape),
            pl.BlockSpec((1, d), lambda i: (0, 0)),
        ],
        out_specs=pl.BlockSpec((tm, d), lambda i: (i, 0)),
        out_shape=jax.ShapeDtypeStruct((t, d), F32),
        compiler_params=_params("parallel"),
        name="ple",
    )(h, p, g, wup, wgate, fg)


def kernel(x, p, ffn1_norm, ffn1_w_gate, ffn1_w_up, ffn1_w_down, mix_norm, w_in, ssm_a_re, ssm_a_im, ssm_log_dt, ssm_b_re, ssm_b_im, ssm_c_re, ssm_c_im, ssm_d, ssm_w_glu, w_branch_ssm, w_branch_attn, w_out, ffn2_norm, ffn2_w_gate, ffn2_w_up, ffn2_w_down, ple_norm, ple_w_up, ple_w_gate, final_norm):
    bsz, seq, d = x.shape
    depth = p.shape[0]
    t = bsz * seq
    d_ssm = ssm_d.shape[-1]
    d_attn = w_branch_attn.shape[-2]
    d_in = w_in.shape[-1]
    cos_t, sin_t = _rotary_tables(seq)
    bf = lambda w: w.astype(BF16)
    row = lambda v: v.reshape(1, -1)

    rows = lambda v: v.reshape(depth, 1, -1)
    ffn1_g, ffn2_g, mix_g = rows(ffn1_norm), rows(ffn2_norm), rows(mix_norm)

    h = x.reshape(t, d)
    for i in range(depth):
        h = _ffn(h, ffn1_g, ffn1_w_gate, ffn1_w_up, ffn1_w_down, i)
        z = _in_proj(h, mix_g, w_in, i)
        z3 = z.reshape(bsz, seq, d_in)
        ssm_w = _ssm_weights(ssm_a_re[i], ssm_a_im[i], ssm_log_dt[i], ssm_b_re[i], ssm_b_im[i],
                             ssm_c_re[i], ssm_c_im[i])
        ya = _ssm(z3, ssm_w, row(ssm_d[i]))
        yb = _moba(z3, cos_t, sin_t, d_ssm=d_ssm, d_attn=d_attn)
        h = _merge(ya.reshape(t, d_ssm), yb.reshape(t, d_attn), z, h,
                   bf(ssm_w_glu[i]), bf(w_branch_ssm[i]), bf(w_branch_attn[i]), bf(w_out[i]))
        h = _ffn(h, ffn2_g, ffn2_w_gate, ffn2_w_up, ffn2_w_down, i)
        h = _ple(h, p[i].reshape(t, -1), row(ple_norm[i]), bf(ple_w_up[i]), bf(ple_w_gate[i]),
                 row(final_norm), final=(i == depth - 1))
    return h.reshape(bsz, seq, d)
```

```python
import functools
import math

import jax
import jax.numpy as jnp
from jax import lax
from jax.experimental import pallas as pl
from jax.experimental.pallas import tpu as pltpu

F32 = jnp.float32
BF16 = jnp.bfloat16

RMS_EPS = 1e-6
SSM_GROUP = 16
SSM_STATE = 64
HEAD_DIM = 128
ROT_DIM = HEAD_DIM // 4
ROPE_THETA = 500000.0
MOBA_BLOCK = 256
MOBA_TOP_K = 3
MOBA_MASK = -1e30

LANES = 128
SUBLANES = 8
VMEM_LIMIT = 56 * 1024 * 1024

SSM_BATCH = 4
SSM_CHUNK = 128
SSM_PITCH = SSM_CHUNK + 4


def _rms(x, g):
    ms = jnp.mean(x * x, axis=-1, keepdims=True)
    return x * lax.rsqrt(ms + RMS_EPS) * g


def _dot(a, b):
    return jnp.dot(a, b, preferred_element_type=F32)


def _dot_nt(a, b):
    return lax.dot_general(a, b, (((1,), (1,)), ((), ())), preferred_element_type=F32)


def _params(*sem):
    return pltpu.CompilerParams(dimension_semantics=sem, vmem_limit_bytes=VMEM_LIMIT)


def _resident(shape):
    nd = len(shape)
    return pl.BlockSpec(shape, lambda *_: (0,) * nd, pipeline_mode=pl.Buffered(1))


def _ffn_kernel(h_ref, g_ref, wg_ref, wu_ref, wd_ref, o_ref, xn_ref):
    j = pl.program_id(1)

    @pl.when(j == 0)
    def _():
        h = h_ref[...]
        xn_ref[...] = _rms(h, g_ref[...]).astype(BF16)
        o_ref[...] = h

    xn = xn_ref[...]
    gate = _dot(xn, wg_ref[...].astype(BF16))
    up = _dot(xn, wu_ref[...].astype(BF16))
    act = (0.5 * gate * jax.nn.sigmoid(gate) * up).astype(BF16)
    o_ref[...] += _dot(act, wd_ref[...].astype(BF16))


def _ffn(h, g, wg, wu, wd, layer, *, tm=1024, tf=256):
    t, d = h.shape
    f = wg.shape[-1]
    assert t % tm == 0 and f % tf == 0
    return pl.pallas_call(
        _ffn_kernel,
        grid=(t // tm, f // tf),
        in_specs=[
            pl.BlockSpec((tm, d), lambda i, j: (i, 0)),
            pl.BlockSpec((None, 1, d), lambda i, j: (layer, 0, 0)),
            pl.BlockSpec((None, d, tf), lambda i, j: (layer, 0, j)),
            pl.BlockSpec((None, d, tf), lambda i, j: (layer, 0, j)),
            pl.BlockSpec((None, tf, d), lambda i, j: (layer, j, 0)),
        ],
        out_specs=pl.BlockSpec((tm, d), lambda i, j: (i, 0)),
        out_shape=jax.ShapeDtypeStruct((t, d), F32),
        scratch_shapes=[pltpu.VMEM((tm, d), BF16)],
        compiler_params=_params("parallel", "arbitrary"),
        name="ffn",
    )(h, g, wg, wu, wd)


def _in_proj_kernel(h_ref, g_ref, w_ref, o_ref, xn_ref):
    @pl.when(pl.program_id(1) == 0)
    def _():
        xn_ref[...] = _rms(h_ref[...], g_ref[...]).astype(BF16)

    o_ref[...] = _dot(xn_ref[...], w_ref[...].astype(BF16))


def _in_proj(h, g, w, layer, *, tm=2048, tn=512):
    t, d = h.shape
    n = w.shape[-1]
    assert t % tm == 0 and n % tn == 0
    return pl.pallas_call(
        _in_proj_kernel,
        grid=(t // tm, n // tn),
        in_specs=[
            pl.BlockSpec((tm, d), lambda i, j: (i, 0), pipeline_mode=pl.Buffered(1)),
            pl.BlockSpec((None, 1, d), lambda i, j: (layer, 0, 0)),
            pl.BlockSpec((None, d, tn), lambda i, j: (layer, 0, j)),
        ],
        out_specs=pl.BlockSpec((tm, tn), lambda i, j: (i, j)),
        out_shape=jax.ShapeDtypeStruct((t, n), F32),
        scratch_shapes=[pltpu.VMEM((tm, d), BF16)],
        compiler_params=_params("parallel", "arbitrary"),
        name="in_proj",
    )(h, g, w)


def _gelu_tanh(x):
    c = math.sqrt(2.0 / math.pi)
    return 0.5 * x * (1.0 + jnp.tanh(c * (x + 0.044715 * (x * x * x))))


def _ssm_kernel(z_ref, wre_ref, wim_ref, wc_ref, lr_ref, li_ref, d_ref, o_ref, bu_ref, st_ref):
    tc = z_ref.shape[1]
    pitch = SSM_PITCH
    n_slabs = bu_ref.shape[0]
    slabs_per_kt = wre_ref.shape[2] // LANES
    n_kt = wre_ref.shape[0]

    @pl.when(pl.program_id(0) == 0)
    def _():
        st_ref[...] = jnp.zeros_like(st_ref)

    for b in range(SSM_BATCH):
        for kt in range(n_kt):
            u = z_ref[b, :, kt * LANES:(kt + 1) * LANES].astype(BF16)
            r_re = _dot(u, wre_ref[kt])
            r_im = _dot(u, wim_ref[kt])
            for q in range(slabs_per_kt):
                s = kt * slabs_per_kt + q
                bu_ref[s, pl.ds(b * pitch, tc), :] = r_re[:, q * LANES:(q + 1) * LANES]
                bu_ref[s, pl.ds((SSM_BATCH + b) * pitch, tc), :] = r_im[:, q * LANES:(q + 1) * LANES]

    group = 8
    for g0 in range(0, n_slabs, group):
        slabs = list(range(g0, g0 + group))
        lrs = [lr_ref[s] for s in slabs]
        lis = [li_ref[s] for s in slabs]

        def body(t, carry, slabs=slabs, lrs=lrs, lis=lis):
            new = []
            for idx, s in enumerate(slabs):
                v = bu_ref[s, pl.ds(t, SUBLANES, stride=pitch), :]
                st = carry[idx]
                st = st * lrs[idx] + pltpu.roll(st, SSM_BATCH, axis=0) * lis[idx] + v
                bu_ref[s, pl.ds(t, SUBLANES, stride=pitch), :] = st
                new.append(st)
            return tuple(new)

        fin = lax.fori_loop(0, tc, body, tuple(st_ref[s] for s in slabs), unroll=4)
        for idx, s in enumerate(slabs):
            st_ref[s] = fin[idx]

    n_nt = wc_ref.shape[0]
    slabs_per_nt = n_slabs // n_nt
    for b in range(SSM_BATCH):
        for nt in range(n_nt):
            parts = [bu_ref[nt * slabs_per_nt + q, pl.ds(b * pitch, tc), :] for q in range(slabs_per_nt)]
            parts += [bu_ref[nt * slabs_per_nt + q, pl.ds((SSM_BATCH + b) * pitch, tc), :]
                      for q in range(slabs_per_nt)]
            lhs = jnp.concatenate(parts, axis=1).astype(BF16)
            y = _dot(lhs, wc_ref[nt])
            cols = slice(nt * LANES, (nt + 1) * LANES)
            y = y + d_ref[:, cols] * z_ref[b, :, cols]
            o_ref[b, :, cols] = _gelu_tanh(y)


def _ssm_weights(a_re, a_im, log_dt, b_re, b_im, c_re, c_im):
    g, n = a_re.shape
    p = b_re.shape[-1]
    dt = jnp.exp(log_dt)[:, None]
    mag = jnp.exp(a_re * dt)
    lam_re = mag * jnp.cos(a_im * dt)
    lam_im = mag * jnp.sin(a_im * dt)
    den = a_re * a_re + a_im * a_im
    num_re = lam_re - 1.0
    coef_re = (num_re * a_re + lam_im * a_im) / den
    coef_im = (lam_im * a_re - num_re * a_im) / den
    bb_re = coef_re[..., None] * b_re - coef_im[..., None] * b_im
    bb_im = coef_re[..., None] * b_im + coef_im[..., None] * b_re

    gpt = LANES // p
    n_t = g // gpt
    eye = jnp.eye(gpt, dtype=F32)

    def in_tiles(bb):
        bb = bb.reshape(n_t, gpt, n, p)
        w = jnp.einsum("ab,tanp->tapbn", eye, bb)
        return w.reshape(n_t, gpt * p, gpt * n).astype(BF16)

    def out_tiles(c):
        c = c.reshape(n_t, gpt, p, n)
        w = jnp.einsum("ab,tapn->tanbp", eye, c)
        return w.reshape(n_t, gpt * n, gpt * p)

    wc = jnp.concatenate([out_tiles(c_re), -out_tiles(c_im)], axis=1).astype(BF16)
    n_slabs = g * n // LANES
    sign = jnp.concatenate([-jnp.ones((SSM_BATCH, 1), F32), jnp.ones((SSM_BATCH, 1), F32)], axis=0)
    lr = jnp.broadcast_to(lam_re.reshape(n_slabs, 1, LANES), (n_slabs, SUBLANES, LANES))
    li = lam_im.reshape(n_slabs, 1, LANES) * sign[None]
    return in_tiles(bb_re), in_tiles(bb_im), wc, lr, li


def _ssm(z3, ssm_w, d_skip):
    bsz, seq, _ = z3.shape
    wre, wim, wc, lr, li = ssm_w
    d_ssm = d_skip.shape[-1]
    n_slabs = lr.shape[0]
    tc = SSM_CHUNK
    assert bsz == SSM_BATCH and seq % tc == 0 and 2 * SSM_BATCH == SUBLANES
    return pl.pallas_call(
        _ssm_kernel,
        grid=(seq // tc,),
        in_specs=[
            pl.BlockSpec((bsz, tc, d_ssm), lambda c: (0, c, 0)),
            _resident(wre.shape), _resident(wim.shape), _resident(wc.shape),
            _resident(lr.shape), _resident(li.shape), _resident(d_skip.shape),
        ],
        out_specs=pl.BlockSpec((bsz, tc, d_ssm), lambda c: (0, c, 0)),
        out_shape=jax.ShapeDtypeStruct((bsz, seq, d_ssm), F32),
        scratch_shapes=[
            pltpu.VMEM((n_slabs, SUBLANES * SSM_PITCH, LANES), F32),
            pltpu.VMEM((n_slabs, SUBLANES, LANES), F32),
        ],
        compiler_params=_params("arbitrary"),
        name="ssm",
    )(z3, wre, wim, wc, lr, li, d_skip)


def _rotary_tables(seq_len):
    pos = jnp.arange(seq_len, dtype=F32)
    inv_freq = 1.0 / (ROPE_THETA ** (jnp.arange(0, ROT_DIM, 2, dtype=F32) / ROT_DIM))
    ang = pos[:, None] * inv_freq[None, :]
    cos, sin = jnp.cos(ang), jnp.sin(ang)
    rest = HEAD_DIM - ROT_DIM
    cos_t = jnp.concatenate([cos, cos, jnp.ones((seq_len, rest), F32)], axis=1)
    sin_t = jnp.concatenate([-sin, sin, jnp.zeros((seq_len, rest), F32)], axis=1)
    return cos_t, sin_t


def _split_bf16(x):
    hi = x.astype(BF16)
    lo = (x - hi.astype(F32)).astype(BF16)
    return hi, lo


def _moba_kernel(q_ref, k_ref, v_ref, cos_ref, sin_ref, o_ref,
                 qf_ref, ka_ref, vt_ref, km_ref, bias_ref, s_ref, p_ref):
    seq = q_ref.shape[0]
    blk = MOBA_BLOCK
    nb = seq // blk
    half = ROT_DIM // 2
    qk_scale = math.log2(math.e) / math.sqrt(HEAD_DIM)
    lane = lax.broadcasted_iota(jnp.int32, (blk, HEAD_DIM), 1)

    def rot(x, rows):
        swapped = jnp.where(lane < half,
                            pltpu.roll(x, HEAD_DIM - half, axis=1),
                            pltpu.roll(x, half, axis=1))
        return x * cos_ref[rows, :] + swapped * sin_ref[rows, :]

    for j in range(nb):
        rows = slice(j * blk, (j + 1) * blk)
        qf_ref[rows, :] = rot(q_ref[rows, :], rows) * qk_scale
        kr = rot(k_ref[rows, :], rows)
        ka_ref[rows, 0:HEAD_DIM] = kr.astype(BF16)
        ka_ref[rows, HEAD_DIM:] = (lane == j).astype(BF16)
        km_ref[j:j + 1, :] = jnp.mean(kr, axis=0, keepdims=True)
        vt_ref[:, rows] = v_ref[rows, :].T.astype(BF16)
    bias_ref[...] = jnp.zeros_like(bias_ref)

    km_hi, km_lo = _split_bf16(km_ref[...])
    blk_id = lax.broadcasted_iota(jnp.int32, (nb, blk), 0)
    key_pos = lax.broadcasted_iota(jnp.int32, (blk, blk), 0)
    qry_pos = lax.broadcasted_iota(jnp.int32, (blk, blk), 1)

    def scores(i):
        buf = i % 2
        q_hi, q_lo = _split_bf16(qf_ref[i * blk:(i + 1) * blk, :])
        n_keys = (i + 1) * blk
        if i > MOBA_TOP_K:
            gate = _dot_nt(km_hi, q_hi) + _dot_nt(km_hi, q_lo) + _dot_nt(km_lo, q_hi)
            past = blk_id < i
            for j in range(i):
                gj = gate[j:j + 1, :]
                beats = ((gate > gj) | ((gate == gj) & (blk_id < j))) & past
                cnt = jnp.sum(beats.astype(F32), axis=0, keepdims=True)
                bias_ref[buf, j:j + 1, :] = jnp.where(cnt < MOBA_TOP_K, 0.0, MOBA_MASK)
            q_aug = jnp.concatenate([q_hi, bias_ref[buf].T.astype(BF16)], axis=1)
            s = _dot_nt(ka_ref[0:n_keys, :], q_aug)
        else:
            s = _dot_nt(ka_ref[0:n_keys, 0:HEAD_DIM], q_hi)
        if i > 0:
            s_ref[buf, 0:i * blk, :] = s[0:i * blk, :]
        s_ref[buf, i * blk:n_keys, :] = jnp.where(key_pos <= qry_pos, s[i * blk:n_keys, :], MOBA_MASK)

    def attend(i):
        buf = i % 2
        n_keys = (i + 1) * blk
        m = jnp.max(s_ref[buf, 0:n_keys, :], axis=0, keepdims=True)
        p = jnp.exp2(s_ref[buf, 0:n_keys, :] - m)
        den = jnp.sum(p, axis=0, keepdims=True)
        p_ref[buf, 0:n_keys, :] = p.astype(BF16)
        out_t = _dot(vt_ref[:, 0:n_keys], p_ref[buf, 0:n_keys, :]) / den
        o_ref[i * blk:(i + 1) * blk, :] = out_t.T

    scores(0)
    for i in range(nb):
        if i + 1 < nb:
            scores(i + 1)
        attend(i)


def _moba(z3, cos_t, sin_t, *, d_ssm, d_attn):
    bsz, seq, _ = z3.shape
    n_heads = d_attn // HEAD_DIM
    assert seq % MOBA_BLOCK == 0
    q0 = d_ssm // HEAD_DIM
    k0 = q0 + n_heads
    v0 = k0 + n_heads
    head_block = lambda c0: pl.BlockSpec((None, seq, HEAD_DIM), lambda b, h: (b, 0, c0 + h))
    return pl.pallas_call(
        _moba_kernel,
        grid=(bsz, n_heads),
        in_specs=[head_block(q0), head_block(k0), head_block(v0),
                  _resident(cos_t.shape), _resident(sin_t.shape)],
        out_specs=pl.BlockSpec((None, seq, HEAD_DIM), lambda b, h: (b, 0, h)),
        out_shape=jax.ShapeDtypeStruct((bsz, seq, d_attn), F32),
        scratch_shapes=[
            pltpu.VMEM((seq, HEAD_DIM), F32),
            pltpu.VMEM((seq, 2 * HEAD_DIM), BF16),
            pltpu.VMEM((HEAD_DIM, seq), BF16),
            pltpu.VMEM((seq // MOBA_BLOCK, HEAD_DIM), F32),
            pltpu.VMEM((2, HEAD_DIM, MOBA_BLOCK), F32),
            pltpu.VMEM((2, seq, MOBA_BLOCK), F32),
            pltpu.VMEM((2, seq, MOBA_BLOCK), BF16),
        ],
        compiler_params=_params("parallel", "parallel"),
        name="moba",
    )(z3, z3, z3, cos_t, sin_t)


def _merge_kernel(ya_ref, yb_ref, gs_ref, ga_ref, h_ref, wglu_ref, wa_ref, wb_ref, wo_ref, o_ref):
    d_ssm = ya_ref.shape[1]
    glu = _dot(ya_ref[...].astype(BF16), wglu_ref[...])
    a = (glu[:, :d_ssm] * jax.nn.sigmoid(glu[:, d_ssm:])).astype(BF16)
    merged = (jax.nn.sigmoid(gs_ref[...]) * _dot(a, wa_ref[...])
              + jax.nn.sigmoid(ga_ref[...]) * _dot(yb_ref[...].astype(BF16), wb_ref[...]))
    o_ref[...] = h_ref[...] + _dot(merged.astype(BF16), wo_ref[...])


def _merge(ya, yb, z, h, wglu, wa, wb, wo, *, tm=256):
    t, d = h.shape
    d_ssm = ya.shape[1]
    d_attn = yb.shape[1]
    gs_blk = (d_ssm + 3 * d_attn) // d
    assert (d_ssm + 3 * d_attn) % d == 0 and t % tm == 0
    return pl.pallas_call(
        _merge_kernel,
        grid=(t // tm,),
        in_specs=[
            pl.BlockSpec((tm, d_ssm), lambda i: (i, 0)),
            pl.BlockSpec((tm, d_attn), lambda i: (i, 0)),
            pl.BlockSpec((tm, d), lambda i: (i, gs_blk)),
            pl.BlockSpec((tm, d), lambda i: (i, gs_blk + 1)),
            pl.BlockSpec((tm, d), lambda i: (i, 0)),
            _resident(wglu.shape), _resident(wa.shape), _resident(wb.shape), _resident(wo.shape),
        ],
        out_specs=pl.BlockSpec((tm, d), lambda i: (i, 0)),
        out_shape=jax.ShapeDtypeStruct((t, d), F32),
        compiler_params=_params("parallel"),
        name="merge",
    )(ya, yb, z, z, h, wglu, wa, wb, wo)


def _ple_kernel(h_ref, p_ref, g_ref, wup_ref, wgate_ref, fg_ref, o_ref, *, final):
    h = h_ref[...]
    gate = jax.nn.sigmoid(_dot(_rms(h, g_ref[...]).astype(BF16), wgate_ref[...]))
    h = h + _dot(p_ref[...].astype(BF16), wup_ref[...]) * gate
    if final:
        h = _rms(h, fg_ref[...])
    o_ref[...] = h


def _ple(h, p, g, wup, wgate, fg, *, final, tm=512):
    t, d = h.shape
    dp = p.shape[1]
    assert t % tm == 0
    return pl.pallas_call(
        functools.partial(_ple_kernel, final=final),
        grid=(t // tm,),
        in_specs=[
            pl.BlockSpec((tm, d), lambda i: (i, 0)),
            pl.BlockSpec((tm, dp), lambda i: (i, 0)),
            pl.BlockSpec((1, d), lambda i: (0, 0)),
            _resident(wup.shape), _resident(wgate.shape),
            pl.BlockSpec((1, d), lambda i: (0, 0)),
        ],
        out_specs=pl.BlockSpec((tm, d), lambda i: (i, 0)),
        out_shape=jax.ShapeDtypeStruct((t, d), F32),
        compiler_params=_params("parallel"),
        name="ple",
    )(h, p, g, wup, wgate, fg)


def kernel(x, p, ffn1_norm, ffn1_w_gate, ffn1_w_up, ffn1_w_down, mix_norm, w_in, ssm_a_re, ssm_a_im, ssm_log_dt, ssm_b_re, ssm_b_im, ssm_c_re, ssm_c_im, ssm_d, ssm_w_glu, w_branch_ssm, w_branch_attn, w_out, ffn2_norm, ffn2_w_gate, ffn2_w_up, ffn2_w_down, ple_norm, ple_w_up, ple_w_gate, final_norm):
    bsz, seq, d = x.shape
    depth = p.shape[0]
    t = bsz * seq
    d_ssm = ssm_d.shape[-1]
    d_attn = w_branch_attn.shape[-2]
    d_in = w_in.shape[-1]
    cos_t, sin_t = _rotary_tables(seq)
    bf = lambda w: w.astype(BF16)
    row = lambda v: v.reshape(1, -1)

    rows = lambda v: v.reshape(depth, 1, -1)
    ffn1_g, ffn2_g, mix_g = rows(ffn1_norm), rows(ffn2_norm), rows(mix_norm)

    h = x.reshape(t, d)
    for i in range(depth):
        h = _ffn(h, ffn1_g, ffn1_w_gate, ffn1_w_up, ffn1_w_down, i)
        z = _in_proj(h, mix_g, w_in, i)
        z3 = z.reshape(bsz, seq, d_in)
        ssm_w = _ssm_weights(ssm_a_re[i], ssm_a_im[i], ssm_log_dt[i], ssm_b_re[i], ssm_b_im[i],
                             ssm_c_re[i], ssm_c_im[i])
        ya = _ssm(z3, ssm_w, row(ssm_d[i]))
        yb = _moba(z3, cos_t, sin_t, d_ssm=d_ssm, d_attn=d_attn)
        h = _merge(ya.reshape(t, d_ssm), yb.reshape(t, d_attn), z, h,
                   bf(ssm_w_glu[i]), bf(w_branch_ssm[i]), bf(w_branch_attn[i]), bf(w_out[i]))
        h = _ffn(h, ffn2_g, ffn2_w_gate, ffn2_w_up, ffn2_w_down, i)
        h = _ple(h, p[i].reshape(t, -1), row(ple_norm[i]), bf(ple_w_up[i]), bf(ple_w_gate[i]),
                 row(final_norm), final=(i == depth - 1))
    return h.reshape(bsz, seq, d)
```

```python
import functools
import math

import jax
import jax.numpy as jnp
from jax import lax
from jax.experimental import pallas as pl
from jax.experimental.pallas import tpu as pltpu

F32 = jnp.float32
BF16 = jnp.bfloat16

RMS_EPS = 1e-6
SSM_GROUP = 16
SSM_STATE = 64
HEAD_DIM = 128
ROT_DIM = HEAD_DIM // 4
ROPE_THETA = 500000.0
MOBA_BLOCK = 256
MOBA_TOP_K = 3
MOBA_HEADS = 2
MOBA_MASK = -1e30

LANES = 128
SUBLANES = 8
VMEM_LIMIT = 56 * 1024 * 1024

SSM_BATCH = 4
SSM_CHUNK = 128
SSM_PITCH = SSM_CHUNK + 4


def _rms(x, g):
    ms = jnp.mean(x * x, axis=-1, keepdims=True)
    return x * lax.rsqrt(ms + RMS_EPS) * g


def _dot(a, b):
    return jnp.dot(a, b, preferred_element_type=F32)


def _dot_nt(a, b):
    return lax.dot_general(a, b, (((1,), (1,)), ((), ())), preferred_element_type=F32)


def _params(*sem):
    return pltpu.CompilerParams(dimension_semantics=sem, vmem_limit_bytes=VMEM_LIMIT)


def _resident(shape):
    nd = len(shape)
    return pl.BlockSpec(shape, lambda *_: (0,) * nd, pipeline_mode=pl.Buffered(1))


def _layer_resident(stacked_shape, layer):
    nd = len(stacked_shape) - 1
    return pl.BlockSpec((None,) + tuple(stacked_shape[1:]), lambda *_: (layer,) + (0,) * nd,
                        pipeline_mode=pl.Buffered(1))


def _ffn_kernel(h_ref, g_ref, wg_ref, wu_ref, wd_ref, o_ref, xn_ref):
    j = pl.program_id(1)

    @pl.when(j == 0)
    def _():
        h = h_ref[...]
        xn_ref[...] = _rms(h, g_ref[...]).astype(BF16)
        o_ref[...] = h

    xn = xn_ref[...]
    gate = _dot(xn, wg_ref[...].astype(BF16))
    up = _dot(xn, wu_ref[...].astype(BF16))
    act = (0.5 * gate * jax.nn.sigmoid(gate) * up).astype(BF16)
    o_ref[...] += _dot(act, wd_ref[...].astype(BF16))


def _ffn(h, g, wg, wu, wd, layer, *, tm=1024, tf=256):
    t, d = h.shape
    f = wg.shape[-1]
    assert t % tm == 0 and f % tf == 0
    return pl.pallas_call(
        _ffn_kernel,
        grid=(t // tm, f // tf),
        in_specs=[
            pl.BlockSpec((tm, d), lambda i, j: (i, 0)),
            pl.BlockSpec((None, 1, d), lambda i, j: (layer, 0, 0)),
            pl.BlockSpec((None, d, tf), lambda i, j: (layer, 0, j)),
            pl.BlockSpec((None, d, tf), lambda i, j: (layer, 0, j)),
            pl.BlockSpec((None, tf, d), lambda i, j: (layer, j, 0)),
        ],
        out_specs=pl.BlockSpec((tm, d), lambda i, j: (i, 0)),
        out_shape=jax.ShapeDtypeStruct((t, d), F32),
        scratch_shapes=[pltpu.VMEM((tm, d), BF16)],
        compiler_params=_params("parallel", "arbitrary"),
        name="ffn",
    )(h, g, wg, wu, wd)


def _in_proj_kernel(h_ref, g_ref, w_ref, o_ref, xn_ref):
    @pl.when(pl.program_id(1) == 0)
    def _():
        xn_ref[...] = _rms(h_ref[...], g_ref[...]).astype(BF16)

    o_ref[...] = _dot(xn_ref[...], w_ref[...].astype(BF16))


def _in_proj(h, g, w, layer, *, tm=2048, tn=512):
    t, d = h.shape
    n = w.shape[-1]
    assert t % tm == 0 and n % tn == 0
    return pl.pallas_call(
        _in_proj_kernel,
        grid=(t // tm, n // tn),
        in_specs=[
            pl.BlockSpec((tm, d), lambda i, j: (i, 0), pipeline_mode=pl.Buffered(1)),
            pl.BlockSpec((None, 1, d), lambda i, j: (layer, 0, 0)),
            pl.BlockSpec((None, d, tn), lambda i, j: (layer, 0, j)),
        ],
        out_specs=pl.BlockSpec((tm, tn), lambda i, j: (i, j)),
        out_shape=jax.ShapeDtypeStruct((t, n), F32),
        scratch_shapes=[pltpu.VMEM((tm, d), BF16)],
        compiler_params=_params("parallel", "arbitrary"),
        name="in_proj",
    )(h, g, w)


def _gelu_tanh(x):
    c = math.sqrt(2.0 / math.pi)
    return 0.5 * x * (1.0 + jnp.tanh(c * (x + 0.044715 * (x * x * x))))


def _ssm_kernel(z_ref, wre_ref, wim_ref, wc_ref, lr_ref, li_ref, d_ref, o_ref, bu_ref, st_ref):
    tc = z_ref.shape[1]
    pitch = SSM_PITCH
    n_pairs = bu_ref.shape[1]
    n_slabs = 2 * n_pairs
    slabs_per_kt = wre_ref.shape[2] // LANES
    n_kt = wre_ref.shape[0]

    def rows_of(s, b):
        return pl.ds(((s // n_pairs) * SSM_BATCH + b) * pitch, tc)

    @pl.when(pl.program_id(0) == 0)
    def _():
        st_ref[...] = jnp.zeros_like(st_ref)

    for b in range(SSM_BATCH):
        for kt in range(n_kt):
            u = z_ref[b, :, kt * LANES:(kt + 1) * LANES].astype(BF16)
            r_re = _dot(u, wre_ref[kt])
            r_im = _dot(u, wim_ref[kt])
            for q in range(slabs_per_kt):
                s = kt * slabs_per_kt + q
                bu_ref[0, s % n_pairs, rows_of(s, b), :] = r_re[:, q * LANES:(q + 1) * LANES]
                bu_ref[1, s % n_pairs, rows_of(s, b), :] = r_im[:, q * LANES:(q + 1) * LANES]

    group = 8
    for g0 in range(0, n_pairs, group):
        pairs = list(range(g0, g0 + group))
        lrs = [lr_ref[p] for p in pairs]
        lis = [li_ref[p] for p in pairs]

        def body(t, carry, pairs=pairs, lrs=lrs, lis=lis):
            new = []
            for idx, p in enumerate(pairs):
                at_t = pl.ds(t, SUBLANES, stride=pitch)
                re, im = carry[idx]
                re_new = re * lrs[idx] - im * lis[idx] + bu_ref[0, p, at_t, :]
                im_new = im * lrs[idx] + re * lis[idx] + bu_ref[1, p, at_t, :]
                bu_ref[0, p, at_t, :] = re_new
                bu_ref[1, p, at_t, :] = im_new
                new.append((re_new, im_new))
            return tuple(new)

        init = tuple((st_ref[0, p], st_ref[1, p]) for p in pairs)
        fin = lax.fori_loop(0, tc, body, init, unroll=2)
        for idx, p in enumerate(pairs):
            st_ref[0, p] = fin[idx][0]
            st_ref[1, p] = fin[idx][1]

    n_nt = wc_ref.shape[0]
    slabs_per_nt = n_slabs // n_nt
    for b in range(SSM_BATCH):
        for nt in range(n_nt):
            slabs = [nt * slabs_per_nt + q for q in range(slabs_per_nt)]
            parts = [bu_ref[0, s % n_pairs, rows_of(s, b), :] for s in slabs]
            parts += [bu_ref[1, s % n_pairs, rows_of(s, b), :] for s in slabs]
            lhs = jnp.concatenate(parts, axis=1).astype(BF16)
            y = _dot(lhs, wc_ref[nt])
            cols = slice(nt * LANES, (nt + 1) * LANES)
            y = y + d_ref[:, cols] * z_ref[b, :, cols]
            o_ref[b, :, cols] = _gelu_tanh(y)


def _ssm_weights(a_re, a_im, log_dt, b_re, b_im, c_re, c_im):
    g, n = a_re.shape
    p = b_re.shape[-1]
    dt = jnp.exp(log_dt)[:, None]
    mag = jnp.exp(a_re * dt)
    lam_re = mag * jnp.cos(a_im * dt)
    lam_im = mag * jnp.sin(a_im * dt)
    den = a_re * a_re + a_im * a_im
    num_re = lam_re - 1.0
    coef_re = (num_re * a_re + lam_im * a_im) / den
    coef_im = (lam_im * a_re - num_re * a_im) / den
    bb_re = coef_re[..., None] * b_re - coef_im[..., None] * b_im
    bb_im = coef_re[..., None] * b_im + coef_im[..., None] * b_re

    gpt = LANES // p
    n_t = g // gpt
    eye = jnp.eye(gpt, dtype=F32)

    def in_tiles(bb):
        bb = bb.reshape(n_t, gpt, n, p)
        w = jnp.einsum("ab,tanp->tapbn", eye, bb)
        return w.reshape(n_t, gpt * p, gpt * n).astype(BF16)

    def out_tiles(c):
        c = c.reshape(n_t, gpt, p, n)
        w = jnp.einsum("ab,tapn->tanbp", eye, c)
        return w.reshape(n_t, gpt * n, gpt * p)

    wc = jnp.concatenate([out_tiles(c_re), -out_tiles(c_im)], axis=1).astype(BF16)
    n_pairs = g * n // LANES // 2

    def pair_tiles(lam):
        lam = lam.reshape(2, n_pairs, 1, LANES)
        lam = jnp.broadcast_to(lam, (2, n_pairs, SSM_BATCH, LANES))
        return jnp.concatenate([lam[0], lam[1]], axis=1)

    return in_tiles(bb_re), in_tiles(bb_im), wc, pair_tiles(lam_re), pair_tiles(lam_im)


def _ssm(z3, ssm_w, d_skip, layer):
    bsz, seq, _ = z3.shape
    wre, wim, wc, lr, li = ssm_w
    d_ssm = d_skip.shape[-1]
    n_pairs = lr.shape[1]
    tc = SSM_CHUNK
    assert bsz == SSM_BATCH and seq % tc == 0 and 2 * SSM_BATCH == SUBLANES
    return pl.pallas_call(
        _ssm_kernel,
        grid=(seq // tc,),
        in_specs=[
            pl.BlockSpec((bsz, tc, d_ssm), lambda c: (0, c, 0)),
            _layer_resident(wre.shape, layer), _layer_resident(wim.shape, layer),
            _layer_resident(wc.shape, layer), _layer_resident(lr.shape, layer),
            _layer_resident(li.shape, layer), _layer_resident(d_skip.shape, layer),
        ],
        out_specs=pl.BlockSpec((bsz, tc, d_ssm), lambda c: (0, c, 0)),
        out_shape=jax.ShapeDtypeStruct((bsz, seq, d_ssm), F32),
        scratch_shapes=[
            pltpu.VMEM((2, n_pairs, SUBLANES * SSM_PITCH, LANES), F32),
            pltpu.VMEM((2, n_pairs, SUBLANES, LANES), F32),
        ],
        compiler_params=_params("arbitrary"),
        name="ssm",
    )(z3, wre, wim, wc, lr, li, d_skip)


def _rotary_tables(seq_len):
    pos = jnp.arange(seq_len, dtype=F32)
    inv_freq = 1.0 / (ROPE_THETA ** (jnp.arange(0, ROT_DIM, 2, dtype=F32) / ROT_DIM))
    ang = pos[:, None] * inv_freq[None, :]
    cos, sin = jnp.cos(ang), jnp.sin(ang)
    rest = HEAD_DIM - ROT_DIM
    cos_t = jnp.concatenate([cos, cos, jnp.ones((seq_len, rest), F32)], axis=1)
    sin_t = jnp.concatenate([-sin, sin, jnp.zeros((seq_len, rest), F32)], axis=1)
    return cos_t, sin_t


def _split_bf16(x):
    hi = x.astype(BF16)
    lo = (x - hi.astype(F32)).astype(BF16)
    return hi, lo


def _moba_kernel(q_ref, k_ref, v_ref, cos_ref, sin_ref, *rest):
    n_w = (len(rest) - 8) // 2
    w_refs, o_ref, wb_refs = rest[:n_w], rest[n_w], rest[n_w + 1:2 * n_w + 1]
    qf_ref, ka_ref, vt_ref, km_ref, bias_ref, s_ref, p_ref = rest[2 * n_w + 1:]
    for w_ref, wb_ref in zip(w_refs, wb_refs):
        wb_ref[...] = w_ref[...].astype(BF16)
    _moba_heads(q_ref, k_ref, v_ref, cos_ref, sin_ref, o_ref,
                qf_ref, ka_ref, vt_ref, km_ref, bias_ref, s_ref, p_ref)


def _moba_heads(q_ref, k_ref, v_ref, cos_ref, sin_ref, o_ref,
                qf_ref, ka_ref, vt_ref, km_ref, bias_ref, s_ref, p_ref):
    seq = q_ref.shape[0]
    n_hd = q_ref.shape[1] // HEAD_DIM
    blk = MOBA_BLOCK
    nb = seq // blk
    half = ROT_DIM // 2
    qk_scale = math.log2(math.e) / math.sqrt(HEAD_DIM)
    lane = lax.broadcasted_iota(jnp.int32, (blk, HEAD_DIM), 1)
    blk_id = lax.broadcasted_iota(jnp.int32, (nb, blk), 0)
    key_pos = lax.broadcasted_iota(jnp.int32, (blk, blk), 0)
    qry_pos = lax.broadcasted_iota(jnp.int32, (blk, blk), 1)

    def rot(x, rows):
        swapped = jnp.where(lane < half,
                            pltpu.roll(x, HEAD_DIM - half, axis=1),
                            pltpu.roll(x, half, axis=1))
        return x * cos_ref[rows, :] + swapped * sin_ref[rows, :]

    for hd in range(n_hd):
        cols = slice(hd * HEAD_DIM, (hd + 1) * HEAD_DIM)
        for j in range(nb):
            rows = slice(j * blk, (j + 1) * blk)
            qf_ref[hd, rows, :] = rot(q_ref[rows, cols], rows) * qk_scale
            kr = rot(k_ref[rows, cols], rows)
            ka_ref[hd, rows, 0:HEAD_DIM] = kr.astype(BF16)
            ka_ref[hd, rows, HEAD_DIM:] = (lane == j).astype(BF16)
            km_ref[hd, j:j + 1, :] = jnp.mean(kr, axis=0, keepdims=True)
            vt_ref[hd, :, rows] = v_ref[rows, cols].T.astype(BF16)
    bias_ref[...] = jnp.zeros_like(bias_ref)

    def scores(hd, i):
        buf = i % 2
        q_hi, q_lo = _split_bf16(qf_ref[hd, i * blk:(i + 1) * blk, :])
        n_keys = (i + 1) * blk
        if i > MOBA_TOP_K:
            km_hi, km_lo = _split_bf16(km_ref[hd])
            gate = _dot_nt(km_hi, q_hi) + _dot_nt(km_hi, q_lo) + _dot_nt(km_lo, q_hi)
            past = blk_id < i
            for j in range(i):
                gj = gate[j:j + 1, :]
                beats = ((gate > gj) | ((gate == gj) & (blk_id < j))) & past
                cnt = jnp.sum(beats.astype(F32), axis=0, keepdims=True)
                bias_ref[hd, buf, j:j + 1, :] = jnp.where(cnt < MOBA_TOP_K, 0.0, MOBA_MASK)
            q_aug = jnp.concatenate([q_hi, bias_ref[hd, buf].T.astype(BF16)], axis=1)
            s = _dot_nt(ka_ref[hd, 0:n_keys, :], q_aug)
        else:
            s = _dot_nt(ka_ref[hd, 0:n_keys, 0:HEAD_DIM], q_hi)
        if i > 0:
            s_ref[hd, buf, 0:i * blk, :] = s[0:i * blk, :]
        s_ref[hd, buf, i * blk:n_keys, :] = jnp.where(key_pos <= qry_pos, s[i * blk:n_keys, :], MOBA_MASK)

    def attend(hd, i):
        buf = i % 2
        n_keys = (i + 1) * blk
        m = jnp.max(s_ref[hd, buf, 0:n_keys, :], axis=0, keepdims=True)
        p = jnp.exp2(s_ref[hd, buf, 0:n_keys, :] - m)
        den = jnp.sum(p, axis=0, keepdims=True)
        p_ref[hd, buf, 0:n_keys, :] = p.astype(BF16)
        out_t = _dot(vt_ref[hd, :, 0:n_keys], p_ref[hd, buf, 0:n_keys, :]) / den
        o_ref[i * blk:(i + 1) * blk, hd * HEAD_DIM:(hd + 1) * HEAD_DIM] = out_t.T

    for hd in range(n_hd):
        scores(hd, 0)
    for i in range(nb):
        if i + 1 < nb:
            for hd in range(n_hd):
                scores(hd, i + 1)
        for hd in range(n_hd):
            attend(hd, i)


def _moba(z3, cos_t, sin_t, stage_w, layer, *, d_ssm, d_attn):
    bsz, seq, _ = z3.shape
    n_heads = d_attn // HEAD_DIM
    n_hd = MOBA_HEADS
    width = n_hd * HEAD_DIM
    assert seq % MOBA_BLOCK == 0 and n_heads % n_hd == 0 and d_ssm % width == 0 and d_attn % width == 0
    q0 = d_ssm // width
    k0 = q0 + n_heads // n_hd
    v0 = k0 + n_heads // n_hd
    head_block = lambda c0: pl.BlockSpec((None, seq, width), lambda b, h: (b, 0, c0 + h))
    n_groups = n_heads // n_hd
    n_steps = bsz * n_groups
    assert all(w.shape[1] % (n_steps * SUBLANES * 2) == 0 for w in stage_w)
    w_in_specs = [pl.BlockSpec((None, w.shape[1] // n_steps, w.shape[2]),
                               lambda b, h: (layer, b * n_groups + h, 0)) for w in stage_w]
    w_out_specs = [pl.BlockSpec((w.shape[1] // n_steps, w.shape[2]),
                                lambda b, h: (b * n_groups + h, 0)) for w in stage_w]
    w_out_shapes = [jax.ShapeDtypeStruct(w.shape[1:], BF16) for w in stage_w]
    return pl.pallas_call(
        _moba_kernel,
        grid=(bsz, n_groups),
        in_specs=[head_block(q0), head_block(k0), head_block(v0),
                  _resident(cos_t.shape), _resident(sin_t.shape)] + w_in_specs,
        out_specs=[pl.BlockSpec((None, seq, width), lambda b, h: (b, 0, h))] + w_out_specs,
        out_shape=[jax.ShapeDtypeStruct((bsz, seq, d_attn), F32)] + w_out_shapes,
        scratch_shapes=[
            pltpu.VMEM((n_hd, seq, HEAD_DIM), F32),
            pltpu.VMEM((n_hd, seq, 2 * HEAD_DIM), BF16),
            pltpu.VMEM((n_hd, HEAD_DIM, seq), BF16),
            pltpu.VMEM((n_hd, seq // MOBA_BLOCK, HEAD_DIM), F32),
            pltpu.VMEM((n_hd, 2, HEAD_DIM, MOBA_BLOCK), F32),
            pltpu.VMEM((n_hd, 2, seq, MOBA_BLOCK), F32),
            pltpu.VMEM((n_hd, 2, seq, MOBA_BLOCK), BF16),
        ],
        compiler_params=_params("parallel", "parallel"),
        name="moba",
    )(z3, z3, z3, cos_t, sin_t, *stage_w)


def _merge_kernel(ya_ref, yb_ref, gs_ref, ga_ref, h_ref, wglu_ref, wa_ref, wb_ref, wo_ref, o_ref):
    d_ssm = ya_ref.shape[1]
    glu = _dot(ya_ref[...].astype(BF16), wglu_ref[...])
    a = (glu[:, :d_ssm] * jax.nn.sigmoid(glu[:, d_ssm:])).astype(BF16)
    merged = (jax.nn.sigmoid(gs_ref[...]) * _dot(a, wa_ref[...])
              + jax.nn.sigmoid(ga_ref[...]) * _dot(yb_ref[...].astype(BF16), wb_ref[...]))
    o_ref[...] = h_ref[...] + _dot(merged.astype(BF16), wo_ref[...])


def _merge(ya, yb, z, h, wglu, wa, wb, wo, *, tm=256):
    t, d = h.shape
    d_ssm = ya.shape[1]
    d_attn = yb.shape[1]
    gs_blk = (d_ssm + 3 * d_attn) // d
    assert (d_ssm + 3 * d_attn) % d == 0 and t % tm == 0
    return pl.pallas_call(
        _merge_kernel,
        grid=(t // tm,),
        in_specs=[
            pl.BlockSpec((tm, d_ssm), lambda i: (i, 0)),
            pl.BlockSpec((tm, d_attn), lambda i: (i, 0)),
            pl.BlockSpec((tm, d), lambda i: (i, gs_blk)),
            pl.BlockSpec((tm, d), lambda i: (i, gs_blk + 1)),
            pl.BlockSpec((tm, d), lambda i: (i, 0)),
            _resident(wglu.shape), _resident(wa.shape), _resident(wb.shape), _resident(wo.shape),
        ],
        out_specs=pl.BlockSpec((tm, d), lambda i: (i, 0)),
        out_shape=jax.ShapeDtypeStruct((t, d), F32),
        compiler_params=_params("parallel"),
        name="merge",
    )(ya, yb, z, z, h, wglu, wa, wb, wo)


def _ple_kernel(h_ref, p_ref, g_ref, wup_ref, wgate_ref, fg_ref, o_ref, *, final):
    h = h_ref[...]
    gate = jax.nn.sigmoid(_dot(_rms(h, g_ref[...]).astype(BF16), wgate_ref[...].astype(BF16)))
    h = h + _dot(p_ref[...].astype(BF16), wup_ref[...].astype(BF16)) * gate
    if final:
        h = _rms(h, fg_ref[...])
    o_ref[...] = h


def _ple(h, p, g, wup, wgate, fg, layer, *, final, tm=512):
    t, d = h.shape
    dp = p.shape[-1]
    assert t % tm == 0
    layer_resident = lambda *blk: pl.BlockSpec((None,) + blk, lambda i: (layer, 0, 0),
                                               pipeline_mode=pl.Buffered(1))
    return pl.pallas_call(
        functools.partial(_ple_kernel, final=final),
        grid=(t // tm,),
        in_specs=[
            pl.BlockSpec((tm, d), lambda i: (i, 0)),
            pl.BlockSpec((None, tm, dp), lambda i: (layer, i, 0)),
            layer_resident(1, d),
            layer_resident(dp, d), layer_resident(d, d),
            pl.BlockSpec((1, d), lambda i: (0, 0)),
        ],
        out_specs=pl.BlockSpec((tm, d), lambda i: (i, 0)),
        out_shape=jax.ShapeDtypeStruct((t, d), F32),
        compiler_params=_params("parallel"),
        name="ple",
    )(h, p, g, wup, wgate, fg)


def kernel(x, p, ffn1_norm, ffn1_w_gate, ffn1_w_up, ffn1_w_down, mix_norm, w_in, ssm_a_re, ssm_a_im, ssm_log_dt, ssm_b_re, ssm_b_im, ssm_c_re, ssm_c_im, ssm_d, ssm_w_glu, w_branch_ssm, w_branch_attn, w_out, ffn2_norm, ffn2_w_gate, ffn2_w_up, ffn2_w_down, ple_norm, ple_w_up, ple_w_gate, final_norm):
    bsz, seq, d = x.shape
    depth = p.shape[0]
    t = bsz * seq
    d_ssm = ssm_d.shape[-1]
    d_attn = w_branch_attn.shape[-2]
    d_in = w_in.shape[-1]
    cos_t, sin_t = _rotary_tables(seq)
    row = lambda v: v.reshape(1, -1)
    rows = lambda v: v.reshape(depth, 1, -1)
    ffn1_g, ffn2_g, mix_g, ple_g = rows(ffn1_norm), rows(ffn2_norm), rows(mix_norm), rows(ple_norm)
    ssm_w = jax.vmap(_ssm_weights)(ssm_a_re, ssm_a_im, ssm_log_dt, ssm_b_re, ssm_b_im, ssm_c_re, ssm_c_im)
    ssm_skip = rows(ssm_d)
    p2 = p.reshape(depth, t, -1)
    merge_w = (ssm_w_glu, w_branch_ssm, w_branch_attn, w_out)

    h = x.reshape(t, d)
    for i in range(depth):
        h = _ffn(h, ffn1_g, ffn1_w_gate, ffn1_w_up, ffn1_w_down, i)
        z = _in_proj(h, mix_g, w_in, i)
        z3 = z.reshape(bsz, seq, d_in)
        ya = _ssm(z3, ssm_w, ssm_skip, i)
        yb, wglu, wa, wb, wo = _moba(z3, cos_t, sin_t, merge_w, i, d_ssm=d_ssm, d_attn=d_attn)
        h = _merge(ya.reshape(t, d_ssm), yb.reshape(t, d_attn), z, h, wglu, wa, wb, wo)
        h = _ffn(h, ffn2_g, ffn2_w_gate, ffn2_w_up, ffn2_w_down, i)
        h = _ple(h, p2, ple_g, ple_w_up, ple_w_gate, row(final_norm), i, final=(i == depth - 1))
    return h.reshape(bsz, seq, d)
```

```python
import functools
import math

import jax
import jax.numpy as jnp
from jax import lax
from jax.experimental import pallas as pl
from jax.experimental.pallas import tpu as pltpu

F32 = jnp.float32
BF16 = jnp.bfloat16

RMS_EPS = 1e-6
SSM_GROUP = 16
SSM_STATE = 64
HEAD_DIM = 128
ROT_DIM = HEAD_DIM // 4
ROPE_THETA = 500000.0
MOBA_BLOCK = 256
MOBA_TOP_K = 3
MOBA_HEADS = 2
MOBA_MASK = -1e30

LANES = 128
SUBLANES = 8
VMEM_LIMIT = 56 * 1024 * 1024

SSM_BATCH = 4
SSM_CHUNK = 128
SSM_PITCH = SSM_CHUNK + 4


def _rms(x, g):
    ms = jnp.mean(x * x, axis=-1, keepdims=True)
    return x * lax.rsqrt(ms + RMS_EPS) * g


def _dot(a, b):
    return jnp.dot(a, b, preferred_element_type=F32)


def _dot_nt(a, b):
    return lax.dot_general(a, b, (((1,), (1,)), ((), ())), preferred_element_type=F32)


def _params(*sem):
    return pltpu.CompilerParams(dimension_semantics=sem, vmem_limit_bytes=VMEM_LIMIT)


def _resident(shape):
    nd = len(shape)
    return pl.BlockSpec(shape, lambda *_: (0,) * nd, pipeline_mode=pl.Buffered(1))


def _layer_resident(stacked_shape, layer):
    nd = len(stacked_shape) - 1
    return pl.BlockSpec((None,) + tuple(stacked_shape[1:]), lambda *_: (layer,) + (0,) * nd,
                        pipeline_mode=pl.Buffered(1))


def _row_tile(h_hbm, hbuf, sem, i, j, n_i):
    tm = hbuf.shape[0]

    def copy(tile):
        return pltpu.make_async_copy(h_hbm.at[pl.ds(tile * tm, tm), :], hbuf, sem)

    @pl.when((i == 0) & (j == 0))
    def _():
        copy(0).start()

    @pl.when(j == 0)
    def _():
        copy(i).wait()

    @pl.when((j == 1) & (i + 1 < n_i))
    def _():
        copy(i + 1).start()


def _ffn_kernel(h_ref, g_ref, wg_ref, wu_ref, wd_ref, o_ref, xn_ref):
    j = pl.program_id(1)

    @pl.when(j == 0)
    def _():
        h = h_ref[...]
        xn_ref[...] = _rms(h, g_ref[...]).astype(BF16)
        o_ref[...] = h

    xn = xn_ref[...]
    gate = _dot(xn, wg_ref[...].astype(BF16))
    up = _dot(xn, wu_ref[...].astype(BF16))
    act = (0.5 * gate * jax.nn.sigmoid(gate) * up).astype(BF16)
    o_ref[...] += _dot(act, wd_ref[...].astype(BF16))


def _ffn(h, g, wg, wu, wd, layer, *, tm=1024, tf=256):
    t, d = h.shape
    f = wg.shape[-1]
    assert t % tm == 0 and f % tf == 0
    return pl.pallas_call(
        _ffn_kernel,
        grid=(t // tm, f // tf),
        in_specs=[
            pl.BlockSpec((tm, d), lambda i, j: (i, 0)),
            pl.BlockSpec((None, 1, d), lambda i, j: (layer, 0, 0)),
            pl.BlockSpec((None, d, tf), lambda i, j: (layer, 0, j)),
            pl.BlockSpec((None, d, tf), lambda i, j: (layer, 0, j)),
            pl.BlockSpec((None, tf, d), lambda i, j: (layer, j, 0)),
        ],
        out_specs=pl.BlockSpec((tm, d), lambda i, j: (i, 0)),
        out_shape=jax.ShapeDtypeStruct((t, d), F32),
        scratch_shapes=[pltpu.VMEM((tm, d), BF16)],
        compiler_params=_params("parallel", "arbitrary"),
        name="ffn",
    )(h, g, wg, wu, wd)


def _in_proj_kernel(h_hbm, g_ref, w_ref, o_ref, hbuf, xn_ref, sem):
    i, j = pl.program_id(0), pl.program_id(1)
    _row_tile(h_hbm, hbuf, sem, i, j, pl.num_programs(0))

    @pl.when(j == 0)
    def _():
        xn_ref[...] = _rms(hbuf[...], g_ref[...]).astype(BF16)

    o_ref[...] = _dot(xn_ref[...], w_ref[...].astype(BF16))


def _in_proj(h, g, w, layer, *, tm=2048, tn=512):
    t, d = h.shape
    n = w.shape[-1]
    assert t % tm == 0 and n % tn == 0 and n // tn >= 2
    return pl.pallas_call(
        _in_proj_kernel,
        grid=(t // tm, n // tn),
        in_specs=[
            pl.BlockSpec(memory_space=pl.ANY),
            pl.BlockSpec((None, 1, d), lambda i, j: (layer, 0, 0)),
            pl.BlockSpec((None, d, tn), lambda i, j: (layer, 0, j)),
        ],
        out_specs=pl.BlockSpec((tm, tn), lambda i, j: (i, j)),
        out_shape=jax.ShapeDtypeStruct((t, n), F32),
        scratch_shapes=[pltpu.VMEM((tm, d), F32), pltpu.VMEM((tm, d), BF16), pltpu.SemaphoreType.DMA(())],
        compiler_params=_params("arbitrary", "arbitrary"),
        name="in_proj",
    )(h, g, w)


def _gelu_tanh(x):
    c = math.sqrt(2.0 / math.pi)
    return 0.5 * x * (1.0 + jnp.tanh(c * (x + 0.044715 * (x * x * x))))


def _ssm_kernel(z_ref, wre_ref, wim_ref, wc_ref, lr_ref, li_ref, d_ref, o_ref, bu_ref, st_ref):
    tc = z_ref.shape[1]
    pitch = SSM_PITCH
    n_pairs = bu_ref.shape[1]
    n_slabs = 2 * n_pairs
    slabs_per_kt = wre_ref.shape[2] // LANES
    n_kt = wre_ref.shape[0]

    def rows_of(s, b):
        return pl.ds(((s // n_pairs) * SSM_BATCH + b) * pitch, tc)

    @pl.when(pl.program_id(0) == 0)
    def _():
        st_ref[...] = jnp.zeros_like(st_ref)

    for b in range(SSM_BATCH):
        for kt in range(n_kt):
            u = z_ref[b, :, kt * LANES:(kt + 1) * LANES].astype(BF16)
            r_re = _dot(u, wre_ref[kt])
            r_im = _dot(u, wim_ref[kt])
            for q in range(slabs_per_kt):
                s = kt * slabs_per_kt + q
                bu_ref[0, s % n_pairs, rows_of(s, b), :] = r_re[:, q * LANES:(q + 1) * LANES]
                bu_ref[1, s % n_pairs, rows_of(s, b), :] = r_im[:, q * LANES:(q + 1) * LANES]

    group = 8
    for g0 in range(0, n_pairs, group):
        pairs = list(range(g0, g0 + group))
        lrs = [lr_ref[p] for p in pairs]
        lis = [li_ref[p] for p in pairs]

        def body(t, carry, pairs=pairs, lrs=lrs, lis=lis):
            new = []
            for idx, p in enumerate(pairs):
                at_t = pl.ds(t, SUBLANES, stride=pitch)
                re, im = carry[idx]
                re_new = re * lrs[idx] - im * lis[idx] + bu_ref[0, p, at_t, :]
                im_new = im * lrs[idx] + re * lis[idx] + bu_ref[1, p, at_t, :]
                bu_ref[0, p, at_t, :] = re_new
                bu_ref[1, p, at_t, :] = im_new
                new.append((re_new, im_new))
            return tuple(new)

        init = tuple((st_ref[0, p], st_ref[1, p]) for p in pairs)
        fin = lax.fori_loop(0, tc, body, init, unroll=2)
        for idx, p in enumerate(pairs):
            st_ref[0, p] = fin[idx][0]
            st_ref[1, p] = fin[idx][1]

    n_nt = wc_ref.shape[0]
    slabs_per_nt = n_slabs // n_nt
    for b in range(SSM_BATCH):
        for nt in range(n_nt):
            slabs = [nt * slabs_per_nt + q for q in range(slabs_per_nt)]
            parts = [bu_ref[0, s % n_pairs, rows_of(s, b), :] for s in slabs]
            parts += [bu_ref[1, s % n_pairs, rows_of(s, b), :] for s in slabs]
            lhs = jnp.concatenate(parts, axis=1).astype(BF16)
            y = _dot(lhs, wc_ref[nt])
            cols = slice(nt * LANES, (nt + 1) * LANES)
            y = y + d_ref[:, cols] * z_ref[b, :, cols]
            o_ref[b, :, cols] = _gelu_tanh(y)


def _ssm_weights(a_re, a_im, log_dt, b_re, b_im, c_re, c_im):
    g, n = a_re.shape
    p = b_re.shape[-1]
    dt = jnp.exp(log_dt)[:, None]
    mag = jnp.exp(a_re * dt)
    lam_re = mag * jnp.cos(a_im * dt)
    lam_im = mag * jnp.sin(a_im * dt)
    den = a_re * a_re + a_im * a_im
    num_re = lam_re - 1.0
    coef_re = (num_re * a_re + lam_im * a_im) / den
    coef_im = (lam_im * a_re - num_re * a_im) / den
    bb_re = coef_re[..., None] * b_re - coef_im[..., None] * b_im
    bb_im = coef_re[..., None] * b_im + coef_im[..., None] * b_re

    gpt = LANES // p
    n_t = g // gpt
    eye = jnp.eye(gpt, dtype=F32)

    def in_tiles(bb):
        bb = bb.reshape(n_t, gpt, n, p)
        w = jnp.einsum("ab,tanp->tapbn", eye, bb)
        return w.reshape(n_t, gpt * p, gpt * n).astype(BF16)

    def out_tiles(c):
        c = c.reshape(n_t, gpt, p, n)
        w = jnp.einsum("ab,tapn->tanbp", eye, c)
        return w.reshape(n_t, gpt * n, gpt * p)

    wc = jnp.concatenate([out_tiles(c_re), -out_tiles(c_im)], axis=1).astype(BF16)
    n_pairs = g * n // LANES // 2

    def pair_tiles(lam):
        lam = lam.reshape(2, n_pairs, 1, LANES)
        lam = jnp.broadcast_to(lam, (2, n_pairs, SSM_BATCH, LANES))
        return jnp.concatenate([lam[0], lam[1]], axis=1)

    return in_tiles(bb_re), in_tiles(bb_im), wc, pair_tiles(lam_re), pair_tiles(lam_im)


def _ssm(z3, ssm_w, d_skip, layer):
    bsz, seq, _ = z3.shape
    wre, wim, wc, lr, li = ssm_w
    d_ssm = d_skip.shape[-1]
    n_pairs = lr.shape[1]
    tc = SSM_CHUNK
    assert bsz == SSM_BATCH and seq % tc == 0 and 2 * SSM_BATCH == SUBLANES
    return pl.pallas_call(
        _ssm_kernel,
        grid=(seq // tc,),
        in_specs=[
            pl.BlockSpec((bsz, tc, d_ssm), lambda c: (0, c, 0)),
            _layer_resident(wre.shape, layer), _layer_resident(wim.shape, layer),
            _layer_resident(wc.shape, layer), _layer_resident(lr.shape, layer),
            _layer_resident(li.shape, layer), _layer_resident(d_skip.shape, layer),
        ],
        out_specs=pl.BlockSpec((bsz, tc, d_ssm), lambda c: (0, c, 0)),
        out_shape=jax.ShapeDtypeStruct((bsz, seq, d_ssm), F32),
        scratch_shapes=[
            pltpu.VMEM((2, n_pairs, SUBLANES * SSM_PITCH, LANES), F32),
            pltpu.VMEM((2, n_pairs, SUBLANES, LANES), F32),
        ],
        compiler_params=_params("arbitrary"),
        name="ssm",
    )(z3, wre, wim, wc, lr, li, d_skip)


def _rotary_tables(seq_len):
    pos = jnp.arange(seq_len, dtype=F32)
    inv_freq = 1.0 / (ROPE_THETA ** (jnp.arange(0, ROT_DIM, 2, dtype=F32) / ROT_DIM))
    ang = pos[:, None] * inv_freq[None, :]
    cos, sin = jnp.cos(ang), jnp.sin(ang)
    rest = HEAD_DIM - ROT_DIM
    cos_t = jnp.concatenate([cos, cos, jnp.ones((seq_len, rest), F32)], axis=1)
    sin_t = jnp.concatenate([-sin, sin, jnp.zeros((seq_len, rest), F32)], axis=1)
    return cos_t, sin_t


def _split_bf16(x):
    hi = x.astype(BF16)
    lo = (x - hi.astype(F32)).astype(BF16)
    return hi, lo


def _moba_kernel(q_ref, k_ref, v_ref, cos_ref, sin_ref, *rest):
    n_w = (len(rest) - 8) // 2
    w_refs, o_ref, wb_refs = rest[:n_w], rest[n_w], rest[n_w + 1:2 * n_w + 1]
    qf_ref, ka_ref, vt_ref, km_ref, bias_ref, s_ref, p_ref = rest[2 * n_w + 1:]
    for w_ref, wb_ref in zip(w_refs, wb_refs):
        wb_ref[...] = w_ref[...].astype(BF16)
    _moba_heads(q_ref, k_ref, v_ref, cos_ref, sin_ref, o_ref,
                qf_ref, ka_ref, vt_ref, km_ref, bias_ref, s_ref, p_ref)


def _moba_heads(q_ref, k_ref, v_ref, cos_ref, sin_ref, o_ref,
                qf_ref, ka_ref, vt_ref, km_ref, bias_ref, s_ref, p_ref):
    seq = q_ref.shape[0]
    n_hd = q_ref.shape[1] // HEAD_DIM
    blk = MOBA_BLOCK
    nb = seq // blk
    half = ROT_DIM // 2
    qk_scale = math.log2(math.e) / math.sqrt(HEAD_DIM)
    lane = lax.broadcasted_iota(jnp.int32, (blk, HEAD_DIM), 1)
    blk_id = lax.broadcasted_iota(jnp.int32, (nb, blk), 0)
    key_pos = lax.broadcasted_iota(jnp.int32, (blk, blk), 0)
    qry_pos = lax.broadcasted_iota(jnp.int32, (blk, blk), 1)

    def rot(x, rows):
        swapped = jnp.where(lane < half,
                            pltpu.roll(x, HEAD_DIM - half, axis=1),
                            pltpu.roll(x, half, axis=1))
        return x * cos_ref[rows, :] + swapped * sin_ref[rows, :]

    for hd in range(n_hd):
        cols = slice(hd * HEAD_DIM, (hd + 1) * HEAD_DIM)
        for j in range(nb):
            rows = slice(j * blk, (j + 1) * blk)
            qf_ref[hd, rows, :] = rot(q_ref[rows, cols], rows) * qk_scale
            kr = rot(k_ref[rows, cols], rows)
            ka_ref[hd, rows, 0:HEAD_DIM] = kr.astype(BF16)
            ka_ref[hd, rows, HEAD_DIM:] = (lane == j).astype(BF16)
            km_ref[hd, j:j + 1, :] = jnp.mean(kr, axis=0, keepdims=True)
            vt_ref[hd, :, rows] = v_ref[rows, cols].T.astype(BF16)
    bias_ref[...] = jnp.zeros_like(bias_ref)

    def scores(hd, i):
        buf = i % 2
        q_hi, q_lo = _split_bf16(qf_ref[hd, i * blk:(i + 1) * blk, :])
        n_keys = (i + 1) * blk
        if i > MOBA_TOP_K:
            km_hi, km_lo = _split_bf16(km_ref[hd])
            gate = _dot_nt(km_hi, q_hi) + _dot_nt(km_hi, q_lo) + _dot_nt(km_lo, q_hi)
            past = blk_id < i
            for j in range(i):
                gj = gate[j:j + 1, :]
                beats = ((gate > gj) | ((gate == gj) & (blk_id < j))) & past
                cnt = jnp.sum(beats.astype(F32), axis=0, keepdims=True)
                bias_ref[hd, buf, j:j + 1, :] = jnp.where(cnt < MOBA_TOP_K, 0.0, MOBA_MASK)
            q_aug = jnp.concatenate([q_hi, bias_ref[hd, buf].T.astype(BF16)], axis=1)
            s = _dot_nt(ka_ref[hd, 0:n_keys, :], q_aug)
        else:
            s = _dot_nt(ka_ref[hd, 0:n_keys, 0:HEAD_DIM], q_hi)
        if i > 0:
            s_ref[hd, buf, 0:i * blk, :] = s[0:i * blk, :]
        s_ref[hd, buf, i * blk:n_keys, :] = jnp.where(key_pos <= qry_pos, s[i * blk:n_keys, :], MOBA_MASK)

    def attend(hd, i):
        buf = i % 2
        n_keys = (i + 1) * blk
        m = jnp.max(s_ref[hd, buf, 0:n_keys, :], axis=0, keepdims=True)
        p = jnp.exp2(s_ref[hd, buf, 0:n_keys, :] - m)
        den = jnp.sum(p, axis=0, keepdims=True)
        p_ref[hd, buf, 0:n_keys, :] = p.astype(BF16)
        out_t = _dot(vt_ref[hd, :, 0:n_keys], p_ref[hd, buf, 0:n_keys, :]) / den
        o_ref[i * blk:(i + 1) * blk, hd * HEAD_DIM:(hd + 1) * HEAD_DIM] = out_t.T

    for hd in range(n_hd):
        scores(hd, 0)
    for i in range(nb):
        if i + 1 < nb:
            for hd in range(n_hd):
                scores(hd, i + 1)
        for hd in range(n_hd):
            attend(hd, i)


def _moba(z3, cos_t, sin_t, stage_w, layer, *, d_ssm, d_attn):
    bsz, seq, _ = z3.shape
    n_heads = d_attn // HEAD_DIM
    n_hd = MOBA_HEADS
    width = n_hd * HEAD_DIM
    assert seq % MOBA_BLOCK == 0 and n_heads % n_hd == 0 and d_ssm % width == 0 and d_attn % width == 0
    q0 = d_ssm // width
    k0 = q0 + n_heads // n_hd
    v0 = k0 + n_heads // n_hd
    head_block = lambda c0: pl.BlockSpec((None, seq, width), lambda b, h: (b, 0, c0 + h))
    n_groups = n_heads // n_hd
    n_steps = bsz * n_groups
    assert all(w.shape[1] % (n_steps * SUBLANES * 2) == 0 for w in stage_w)
    w_in_specs = [pl.BlockSpec((None, w.shape[1] // n_steps, w.shape[2]),
                               lambda b, h: (layer, b * n_groups + h, 0)) for w in stage_w]
    w_out_specs = [pl.BlockSpec((w.shape[1] // n_steps, w.shape[2]),
                                lambda b, h: (b * n_groups + h, 0)) for w in stage_w]
    w_out_shapes = [jax.ShapeDtypeStruct(w.shape[1:], BF16) for w in stage_w]
    return pl.pallas_call(
        _moba_kernel,
        grid=(bsz, n_groups),
        in_specs=[head_block(q0), head_block(k0), head_block(v0),
                  _resident(cos_t.shape), _resident(sin_t.shape)] + w_in_specs,
        out_specs=[pl.BlockSpec((None, seq, width), lambda b, h: (b, 0, h))] + w_out_specs,
        out_shape=[jax.ShapeDtypeStruct((bsz, seq, d_attn), F32)] + w_out_shapes,
        scratch_shapes=[
            pltpu.VMEM((n_hd, seq, HEAD_DIM), F32),
            pltpu.VMEM((n_hd, seq, 2 * HEAD_DIM), BF16),
            pltpu.VMEM((n_hd, HEAD_DIM, seq), BF16),
            pltpu.VMEM((n_hd, seq // MOBA_BLOCK, HEAD_DIM), F32),
            pltpu.VMEM((n_hd, 2, HEAD_DIM, MOBA_BLOCK), F32),
            pltpu.VMEM((n_hd, 2, seq, MOBA_BLOCK), F32),
            pltpu.VMEM((n_hd, 2, seq, MOBA_BLOCK), BF16),
        ],
        compiler_params=_params("parallel", "parallel"),
        name="moba",
    )(z3, z3, z3, cos_t, sin_t, *stage_w)


def _merge_kernel(ya_ref, yb_ref, gs_ref, ga_ref, h_ref, wglu_ref, wa_ref, wb_ref, wo_ref, o_ref):
    d_ssm = ya_ref.shape[1]
    glu = _dot(ya_ref[...].astype(BF16), wglu_ref[...])
    a = (glu[:, :d_ssm] * jax.nn.sigmoid(glu[:, d_ssm:])).astype(BF16)
    merged = (jax.nn.sigmoid(gs_ref[...]) * _dot(a, wa_ref[...])
              + jax.nn.sigmoid(ga_ref[...]) * _dot(yb_ref[...].astype(BF16), wb_ref[...]))
    o_ref[...] = h_ref[...] + _dot(merged.astype(BF16), wo_ref[...])


def _merge(ya, yb, z, h, wglu, wa, wb, wo, *, tm=256):
    t, d = h.shape
    d_ssm = ya.shape[1]
    d_attn = yb.shape[1]
    gs_blk = (d_ssm + 3 * d_attn) // d
    assert (d_ssm + 3 * d_attn) % d == 0 and t % tm == 0
    return pl.pallas_call(
        _merge_kernel,
        grid=(t // tm,),
        in_specs=[
            pl.BlockSpec((tm, d_ssm), lambda i: (i, 0)),
            pl.BlockSpec((tm, d_attn), lambda i: (i, 0)),
            pl.BlockSpec((tm, d), lambda i: (i, gs_blk)),
            pl.BlockSpec((tm, d), lambda i: (i, gs_blk + 1)),
            pl.BlockSpec((tm, d), lambda i: (i, 0)),
            _resident(wglu.shape), _resident(wa.shape), _resident(wb.shape), _resident(wo.shape),
        ],
        out_specs=pl.BlockSpec((tm, d), lambda i: (i, 0)),
        out_shape=jax.ShapeDtypeStruct((t, d), F32),
        compiler_params=_params("parallel"),
        name="merge",
    )(ya, yb, z, z, h, wglu, wa, wb, wo)


def _ple_kernel(h_ref, p_ref, g_ref, wup_ref, wgate_ref, fg_ref, o_ref, *, final):
    h = h_ref[...]
    gate = jax.nn.sigmoid(_dot(_rms(h, g_ref[...]).astype(BF16), wgate_ref[...].astype(BF16)))
    h = h + _dot(p_ref[...].astype(BF16), wup_ref[...].astype(BF16)) * gate
    if final:
        h = _rms(h, fg_ref[...])
    o_ref[...] = h


def _ple(h, p, g, wup, wgate, fg, layer, *, final, tm=512):
    t, d = h.shape
    dp = p.shape[-1]
    assert t % tm == 0
    layer_resident = lambda *blk: pl.BlockSpec((None,) + blk, lambda i: (layer, 0, 0),
                                               pipeline_mode=pl.Buffered(1))
    return pl.pallas_call(
        functools.partial(_ple_kernel, final=final),
        grid=(t // tm,),
        in_specs=[
            pl.BlockSpec((tm, d), lambda i: (i, 0)),
            pl.BlockSpec((None, tm, dp), lambda i: (layer, i, 0)),
            layer_resident(1, d),
            layer_resident(dp, d), layer_resident(d, d),
            pl.BlockSpec((1, d), lambda i: (0, 0)),
        ],
        out_specs=pl.BlockSpec((tm, d), lambda i: (i, 0)),
        out_shape=jax.ShapeDtypeStruct((t, d), F32),
        compiler_params=_params("parallel"),
        name="ple",
    )(h, p, g, wup, wgate, fg)


def kernel(x, p, ffn1_norm, ffn1_w_gate, ffn1_w_up, ffn1_w_down, mix_norm, w_in, ssm_a_re, ssm_a_im, ssm_log_dt, ssm_b_re, ssm_b_im, ssm_c_re, ssm_c_im, ssm_d, ssm_w_glu, w_branch_ssm, w_branch_attn, w_out, ffn2_norm, ffn2_w_gate, ffn2_w_up, ffn2_w_down, ple_norm, ple_w_up, ple_w_gate, final_norm):
    bsz, seq, d = x.shape
    depth = p.shape[0]
    t = bsz * seq
    d_ssm = ssm_d.shape[-1]
    d_attn = w_branch_attn.shape[-2]
    d_in = w_in.shape[-1]
    cos_t, sin_t = _rotary_tables(seq)
    row = lambda v: v.reshape(1, -1)
    rows = lambda v: v.reshape(depth, 1, -1)
    ffn1_g, ffn2_g, mix_g, ple_g = rows(ffn1_norm), rows(ffn2_norm), rows(mix_norm), rows(ple_norm)
    ssm_w = jax.vmap(_ssm_weights)(ssm_a_re, ssm_a_im, ssm_log_dt, ssm_b_re, ssm_b_im, ssm_c_re, ssm_c_im)
    ssm_skip = rows(ssm_d)
    p2 = p.reshape(depth, t, -1)
    merge_w = (ssm_w_glu, w_branch_ssm, w_branch_attn, w_out)

    h = x.reshape(t, d)
    for i in range(depth):
        h = _ffn(h, ffn1_g, ffn1_w_gate, ffn1_w_up, ffn1_w_down, i)
        z = _in_proj(h, mix_g, w_in, i)
        z3 = z.reshape(bsz, seq, d_in)
        ya = _ssm(z3, ssm_w, ssm_skip, i)
        yb, wglu, wa, wb, wo = _moba(z3, cos_t, sin_t, merge_w, i, d_ssm=d_ssm, d_attn=d_attn)
        h = _merge(ya.reshape(t, d_ssm), yb.reshape(t, d_attn), z, h, wglu, wa, wb, wo)
        h = _ffn(h, ffn2_g, ffn2_w_gate, ffn2_w_up, ffn2_w_down, i)
        h = _ple(h, p2, ple_g, ple_w_up, ple_w_gate, row(final_norm), i, final=(i == depth - 1))
    return h.reshape(bsz, seq, d)
```

```python
import functools
import math

import jax
import jax.numpy as jnp
from jax import lax
from jax.experimental import pallas as pl
from jax.experimental.pallas import tpu as pltpu

F32 = jnp.float32
BF16 = jnp.bfloat16

RMS_EPS = 1e-6
HEAD_DIM = 128
ROT_DIM = HEAD_DIM // 4
ROPE_THETA = 500000.0
MOBA_BLOCK = 256
MOBA_TOP_K = 3
MOBA_HEADS = 2
MOBA_BUFS = 3
MOBA_MASK = -1e30

LANES = 128
SUBLANES = 8
VMEM_LIMIT = 56 * 1024 * 1024

SSM_BATCH = 4
SSM_CHUNK = 128
SSM_PITCH = SSM_CHUNK + 4


def _rms(x, g):
    ms = jnp.mean(x * x, axis=-1, keepdims=True)
    return x * lax.rsqrt(ms + RMS_EPS) * g


def _dot(a, b):
    return jnp.dot(a, b, preferred_element_type=F32)


def _dot_nt(a, b):
    return lax.dot_general(a, b, (((1,), (1,)), ((), ())), preferred_element_type=F32)


def _params(*sem):
    return pltpu.CompilerParams(dimension_semantics=sem, vmem_limit_bytes=VMEM_LIMIT)


def _resident(shape):
    nd = len(shape)
    return pl.BlockSpec(shape, lambda *_: (0,) * nd, pipeline_mode=pl.Buffered(1))


def _layer_resident(stacked_shape, layer):
    nd = len(stacked_shape) - 1
    return pl.BlockSpec((None,) + tuple(stacked_shape[1:]), lambda *_: (layer,) + (0,) * nd,
                        pipeline_mode=pl.Buffered(1))


def _row_tile(h_hbm, hbuf, sem, i, j, n_i):
    tm = hbuf.shape[0]

    def copy(tile):
        return pltpu.make_async_copy(h_hbm.at[pl.ds(tile * tm, tm), :], hbuf, sem)

    @pl.when((i == 0) & (j == 0))
    def _():
        copy(0).start()

    @pl.when(j == 0)
    def _():
        copy(i).wait()

    @pl.when((j == 1) & (i + 1 < n_i))
    def _():
        copy(i + 1).start()


def _ffn_kernel(h_ref, g_ref, wg_ref, wu_ref, wd_ref, o_ref, xn_ref, act_ref):
    j = pl.program_id(1)
    n_f = pl.num_programs(1) - 1

    def gate_up():
        xn = xn_ref[...]
        gate = _dot(xn, wg_ref[...].astype(BF16))
        up = _dot(xn, wu_ref[...].astype(BF16))
        return (0.5 * gate * jax.nn.sigmoid(gate) * up).astype(BF16)

    def down():
        o_ref[...] += _dot(act_ref[...], wd_ref[...].astype(BF16))

    @pl.when(j == 0)
    def _():
        h = h_ref[...]
        xn_ref[...] = _rms(h, g_ref[...]).astype(BF16)
        o_ref[...] = h
        act_ref[...] = gate_up()

    @pl.when((j > 0) & (j < n_f))
    def _():
        act = gate_up()
        down()
        act_ref[...] = act

    @pl.when(j == n_f)
    def _():
        down()


def _ffn(h, g, wg, wu, wd, layer, *, tm=1024, tf=256):
    t, d = h.shape
    f = wg.shape[-1]
    assert t % tm == 0 and f % tf == 0
    n_f = f // tf
    ahead = lambda i, j: (layer, 0, jnp.minimum(j, n_f - 1))
    behind = lambda i, j: (layer, jnp.maximum(j - 1, 0), 0)
    return pl.pallas_call(
        _ffn_kernel,
        grid=(t // tm, n_f + 1),
        in_specs=[
            pl.BlockSpec((tm, d), lambda i, j: (i, 0)),
            pl.BlockSpec((None, 1, d), lambda i, j: (layer, 0, 0)),
            pl.BlockSpec((None, d, tf), ahead),
            pl.BlockSpec((None, d, tf), ahead),
            pl.BlockSpec((None, tf, d), behind),
        ],
        out_specs=pl.BlockSpec((tm, d), lambda i, j: (i, 0)),
        out_shape=jax.ShapeDtypeStruct((t, d), F32),
        scratch_shapes=[pltpu.VMEM((tm, d), BF16), pltpu.VMEM((tm, tf), BF16)],
        compiler_params=_params("parallel", "arbitrary"),
        name="ffn",
    )(h, g, wg, wu, wd)


def _in_proj_kernel(h_hbm, g_ref, w_ref, o_ref, hbuf, xn_ref, sem):
    i, j = pl.program_id(0), pl.program_id(1)
    _row_tile(h_hbm, hbuf, sem, i, j, pl.num_programs(0))

    @pl.when(j == 0)
    def _():
        xn_ref[...] = _rms(hbuf[...], g_ref[...]).astype(BF16)

    o_ref[...] = _dot(xn_ref[...], w_ref[...].astype(BF16))


def _in_proj(h, g, w, layer, *, tm=2048, tn=512):
    t, d = h.shape
    n = w.shape[-1]
    assert t % tm == 0 and n % tn == 0 and n // tn >= 2
    return pl.pallas_call(
        _in_proj_kernel,
        grid=(t // tm, n // tn),
        in_specs=[
            pl.BlockSpec(memory_space=pl.ANY),
            pl.BlockSpec((None, 1, d), lambda i, j: (layer, 0, 0)),
            pl.BlockSpec((None, d, tn), lambda i, j: (layer, 0, j)),
        ],
        out_specs=pl.BlockSpec((tm, tn), lambda i, j: (i, j)),
        out_shape=jax.ShapeDtypeStruct((t, n), F32),
        scratch_shapes=[pltpu.VMEM((tm, d), F32), pltpu.VMEM((tm, d), BF16), pltpu.SemaphoreType.DMA(())],
        compiler_params=_params("arbitrary", "arbitrary"),
        name="in_proj",
    )(h, g, w)


def _gelu_tanh(x):
    c = math.sqrt(2.0 / math.pi)
    return 0.5 * x * (1.0 + jnp.tanh(c * (x + 0.044715 * (x * x * x))))


def _ssm_kernel(z_ref, wre_ref, wim_ref, wc_ref, lr_ref, li_ref, d_ref, o_ref, bu_ref, st_ref):
    tc = z_ref.shape[1]
    pitch = SSM_PITCH
    n_pairs = bu_ref.shape[1]
    n_slabs = 2 * n_pairs
    slabs_per_kt = wre_ref.shape[2] // LANES
    n_kt = wre_ref.shape[0]

    def rows_of(s, b):
        return pl.ds(((s // n_pairs) * SSM_BATCH + b) * pitch, tc)

    @pl.when(pl.program_id(0) == 0)
    def _():
        st_ref[...] = jnp.zeros_like(st_ref)

    for b in range(SSM_BATCH):
        for kt in range(n_kt):
            u = z_ref[b, :, kt * LANES:(kt + 1) * LANES].astype(BF16)
            r_re = _dot(u, wre_ref[kt])
            r_im = _dot(u, wim_ref[kt])
            for q in range(slabs_per_kt):
                s = kt * slabs_per_kt + q
                bu_ref[0, s % n_pairs, rows_of(s, b), :] = r_re[:, q * LANES:(q + 1) * LANES]
                bu_ref[1, s % n_pairs, rows_of(s, b), :] = r_im[:, q * LANES:(q + 1) * LANES]

    group = 8
    for g0 in range(0, n_pairs, group):
        pairs = list(range(g0, g0 + group))
        lrs = [lr_ref[p] for p in pairs]
        lis = [li_ref[p] for p in pairs]

        def body(t, carry, pairs=pairs, lrs=lrs, lis=lis):
            new = []
            for idx, p in enumerate(pairs):
                at_t = pl.ds(t, SUBLANES, stride=pitch)
                re, im = carry[idx]
                re_new = re * lrs[idx] - im * lis[idx] + bu_ref[0, p, at_t, :]
                im_new = im * lrs[idx] + re * lis[idx] + bu_ref[1, p, at_t, :]
                bu_ref[0, p, at_t, :] = re_new
                bu_ref[1, p, at_t, :] = im_new
                new.append((re_new, im_new))
            return tuple(new)

        init = tuple((st_ref[0, p], st_ref[1, p]) for p in pairs)
        fin = lax.fori_loop(0, tc, body, init, unroll=2)
        for idx, p in enumerate(pairs):
            st_ref[0, p] = fin[idx][0]
            st_ref[1, p] = fin[idx][1]

    n_nt = wc_ref.shape[0]
    slabs_per_nt = n_slabs // n_nt
    for b in range(SSM_BATCH):
        for nt in range(n_nt):
            slabs = [nt * slabs_per_nt + q for q in range(slabs_per_nt)]
            parts = [bu_ref[0, s % n_pairs, rows_of(s, b), :] for s in slabs]
            parts += [bu_ref[1, s % n_pairs, rows_of(s, b), :] for s in slabs]
            lhs = jnp.concatenate(parts, axis=1).astype(BF16)
            y = _dot(lhs, wc_ref[nt])
            cols = slice(nt * LANES, (nt + 1) * LANES)
            y = y + d_ref[:, cols] * z_ref[b, :, cols]
            o_ref[b, :, cols] = _gelu_tanh(y)


def _ssm_weights(a_re, a_im, log_dt, b_re, b_im, c_re, c_im):
    g, n = a_re.shape
    p = b_re.shape[-1]
    dt = jnp.exp(log_dt)[:, None]
    mag = jnp.exp(a_re * dt)
    lam_re = mag * jnp.cos(a_im * dt)
    lam_im = mag * jnp.sin(a_im * dt)
    den = a_re * a_re + a_im * a_im
    num_re = lam_re - 1.0
    coef_re = (num_re * a_re + lam_im * a_im) / den
    coef_im = (lam_im * a_re - num_re * a_im) / den
    bb_re = coef_re[..., None] * b_re - coef_im[..., None] * b_im
    bb_im = coef_re[..., None] * b_im + coef_im[..., None] * b_re

    gpt = LANES // p
    n_t = g // gpt
    eye = jnp.eye(gpt, dtype=F32)

    def in_tiles(bb):
        bb = bb.reshape(n_t, gpt, n, p)
        w = jnp.einsum("ab,tanp->tapbn", eye, bb)
        return w.reshape(n_t, gpt * p, gpt * n).astype(BF16)

    def out_tiles(c):
        c = c.reshape(n_t, gpt, p, n)
        w = jnp.einsum("ab,tapn->tanbp", eye, c)
        return w.reshape(n_t, gpt * n, gpt * p)

    wc = jnp.concatenate([out_tiles(c_re), -out_tiles(c_im)], axis=1).astype(BF16)
    n_pairs = g * n // LANES // 2

    def pair_tiles(lam):
        lam = lam.reshape(2, n_pairs, 1, LANES)
        lam = jnp.broadcast_to(lam, (2, n_pairs, SSM_BATCH, LANES))
        return jnp.concatenate([lam[0], lam[1]], axis=1)

    return in_tiles(bb_re), in_tiles(bb_im), wc, pair_tiles(lam_re), pair_tiles(lam_im)


def _ssm(z3, ssm_w, d_skip, layer):
    bsz, seq, _ = z3.shape
    wre, wim, wc, lr, li = ssm_w
    d_ssm = d_skip.shape[-1]
    n_pairs = lr.shape[1]
    tc = SSM_CHUNK
    assert bsz == SSM_BATCH and seq % tc == 0 and 2 * SSM_BATCH == SUBLANES
    return pl.pallas_call(
        _ssm_kernel,
        grid=(seq // tc,),
        in_specs=[
            pl.BlockSpec((bsz, tc, d_ssm), lambda c: (0, c, 0)),
            _layer_resident(wre.shape, layer), _layer_resident(wim.shape, layer),
            _layer_resident(wc.shape, layer), _layer_resident(lr.shape, layer),
            _layer_resident(li.shape, layer), _layer_resident(d_skip.shape, layer),
        ],
        out_specs=pl.BlockSpec((bsz, tc, d_ssm), lambda c: (0, c, 0)),
        out_shape=jax.ShapeDtypeStruct((bsz, seq, d_ssm), F32),
        scratch_shapes=[
            pltpu.VMEM((2, n_pairs, SUBLANES * SSM_PITCH, LANES), F32),
            pltpu.VMEM((2, n_pairs, SUBLANES, LANES), F32),
        ],
        compiler_params=_params("arbitrary"),
        name="ssm",
    )(z3, wre, wim, wc, lr, li, d_skip)


def _rotary_tables(seq_len):
    pos = jnp.arange(seq_len, dtype=F32)
    inv_freq = 1.0 / (ROPE_THETA ** (jnp.arange(0, ROT_DIM, 2, dtype=F32) / ROT_DIM))
    ang = pos[:, None] * inv_freq[None, :]
    cos, sin = jnp.cos(ang), jnp.sin(ang)
    rest = HEAD_DIM - ROT_DIM
    cos_t = jnp.concatenate([cos, cos, jnp.ones((seq_len, rest), F32)], axis=1)
    sin_t = jnp.concatenate([-sin, sin, jnp.zeros((seq_len, rest), F32)], axis=1)
    return cos_t, sin_t


def _split_bf16(x):
    hi = x.astype(BF16)
    lo = (x - hi.astype(F32)).astype(BF16)
    return hi, lo


def _moba_kernel(q_ref, k_ref, v_ref, cos_ref, sin_ref, *rest):
    n_w = (len(rest) - 8) // 2
    w_refs, o_ref, wb_refs = rest[:n_w], rest[n_w], rest[n_w + 1:2 * n_w + 1]
    qf_ref, ka_ref, vt_ref, km_ref, bias_ref, s_ref, p_ref = rest[2 * n_w + 1:]
    for w_ref, wb_ref in zip(w_refs, wb_refs):
        wb_ref[...] = w_ref[...].astype(BF16)
    _moba_heads(q_ref, k_ref, v_ref, cos_ref, sin_ref, o_ref,
                qf_ref, ka_ref, vt_ref, km_ref, bias_ref, s_ref, p_ref)


def _moba_heads(q_ref, k_ref, v_ref, cos_ref, sin_ref, o_ref,
                qf_ref, ka_ref, vt_ref, km_ref, bias_ref, s_ref, p_ref):
    seq = q_ref.shape[0]
    n_hd = q_ref.shape[1] // HEAD_DIM
    blk = MOBA_BLOCK
    nb = seq // blk
    half = ROT_DIM // 2
    qk_scale = math.log2(math.e) / math.sqrt(HEAD_DIM)
    lane = lax.broadcasted_iota(jnp.int32, (blk, HEAD_DIM), 1)
    blk_id = lax.broadcasted_iota(jnp.int32, (nb, blk), 0)
    key_pos = lax.broadcasted_iota(jnp.int32, (blk, blk), 0)
    qry_pos = lax.broadcasted_iota(jnp.int32, (blk, blk), 1)

    def rot(x, rows):
        swapped = jnp.where(lane < half,
                            pltpu.roll(x, HEAD_DIM - half, axis=1),
                            pltpu.roll(x, half, axis=1))
        return x * cos_ref[rows, :] + swapped * sin_ref[rows, :]

    for hd in range(n_hd):
        cols = slice(hd * HEAD_DIM, (hd + 1) * HEAD_DIM)
        for j in range(nb):
            rows = slice(j * blk, (j + 1) * blk)
            qf_ref[hd, rows, :] = rot(q_ref[rows, cols], rows) * qk_scale
            kr = rot(k_ref[rows, cols], rows)
            ka_ref[hd, rows, 0:HEAD_DIM] = kr.astype(BF16)
            ka_ref[hd, rows, HEAD_DIM:] = (lane == j).astype(BF16)
            km_ref[hd, j:j + 1, :] = jnp.mean(kr, axis=0, keepdims=True)
            vt_ref[hd, :, rows] = v_ref[rows, cols].T.astype(BF16)
    bias_ref[...] = jnp.zeros_like(bias_ref)

    def scores(hd, i):
        buf = i % MOBA_BUFS
        q_hi, q_lo = _split_bf16(qf_ref[hd, i * blk:(i + 1) * blk, :])
        n_keys = (i + 1) * blk
        if i > MOBA_TOP_K:
            km_hi, km_lo = _split_bf16(km_ref[hd])
            gate = _dot_nt(km_hi, q_hi) + _dot_nt(km_hi, q_lo) + _dot_nt(km_lo, q_hi)
            past = blk_id < i
            for j in range(i):
                gj = gate[j:j + 1, :]
                beats = ((gate > gj) | ((gate == gj) & (blk_id < j))) & past
                cnt = jnp.sum(beats.astype(F32), axis=0, keepdims=True)
                bias_ref[hd, buf, j:j + 1, :] = jnp.where(cnt < MOBA_TOP_K, 0.0, MOBA_MASK)
            q_aug = jnp.concatenate([q_hi, bias_ref[hd, buf].T.astype(BF16)], axis=1)
            s = _dot_nt(ka_ref[hd, 0:n_keys, :], q_aug)
        else:
            s = _dot_nt(ka_ref[hd, 0:n_keys, 0:HEAD_DIM], q_hi)
        if i > 0:
            s_ref[hd, buf, 0:i * blk, :] = s[0:i * blk, :]
        s_ref[hd, buf, i * blk:n_keys, :] = jnp.where(key_pos <= qry_pos, s[i * blk:n_keys, :], MOBA_MASK)

    def attend(hd, i):
        buf = i % MOBA_BUFS
        n_keys = (i + 1) * blk
        m = jnp.max(s_ref[hd, buf, 0:n_keys, :], axis=0, keepdims=True)
        p = jnp.exp2(s_ref[hd, buf, 0:n_keys, :] - m)
        den = jnp.sum(p, axis=0, keepdims=True)
        p_ref[hd, buf, 0:n_keys, :] = p.astype(BF16)
        out_t = _dot(vt_ref[hd, :, 0:n_keys], p_ref[hd, buf, 0:n_keys, :]) / den
        o_ref[i * blk:(i + 1) * blk, hd * HEAD_DIM:(hd + 1) * HEAD_DIM] = out_t.T

    ahead = MOBA_BUFS - 1
    for i0 in range(min(ahead, nb)):
        for hd in range(n_hd):
            scores(hd, i0)
    for i in range(nb):
        if i + ahead < nb:
            for hd in range(n_hd):
                scores(hd, i + ahead)
        for hd in range(n_hd):
            attend(hd, i)


def _moba(z3, cos_t, sin_t, stage_w, layer, *, d_ssm, d_attn):
    bsz, seq, _ = z3.shape
    n_heads = d_attn // HEAD_DIM
    n_hd = MOBA_HEADS
    width = n_hd * HEAD_DIM
    assert seq % MOBA_BLOCK == 0 and n_heads % n_hd == 0 and d_ssm % width == 0 and d_attn % width == 0
    q0 = d_ssm // width
    k0 = q0 + n_heads // n_hd
    v0 = k0 + n_heads // n_hd
    head_block = lambda c0: pl.BlockSpec((None, seq, width), lambda b, h: (b, 0, c0 + h))
    n_groups = n_heads // n_hd
    n_steps = bsz * n_groups
    assert all(w.shape[1] % (n_steps * SUBLANES * 2) == 0 for w in stage_w)
    w_in_specs = [pl.BlockSpec((None, w.shape[1] // n_steps, w.shape[2]),
                               lambda b, h: (layer, b * n_groups + h, 0)) for w in stage_w]
    w_out_specs = [pl.BlockSpec((w.shape[1] // n_steps, w.shape[2]),
                                lambda b, h: (b * n_groups + h, 0)) for w in stage_w]
    w_out_shapes = [jax.ShapeDtypeStruct(w.shape[1:], BF16) for w in stage_w]
    return pl.pallas_call(
        _moba_kernel,
        grid=(bsz, n_groups),
        in_specs=[head_block(q0), head_block(k0), head_block(v0),
                  _resident(cos_t.shape), _resident(sin_t.shape)] + w_in_specs,
        out_specs=[pl.BlockSpec((None, seq, width), lambda b, h: (b, 0, h))] + w_out_specs,
        out_shape=[jax.ShapeDtypeStruct((bsz, seq, d_attn), F32)] + w_out_shapes,
        scratch_shapes=[
            pltpu.VMEM((n_hd, seq, HEAD_DIM), F32),
            pltpu.VMEM((n_hd, seq, 2 * HEAD_DIM), BF16),
            pltpu.VMEM((n_hd, HEAD_DIM, seq), BF16),
            pltpu.VMEM((n_hd, seq // MOBA_BLOCK, HEAD_DIM), F32),
            pltpu.VMEM((n_hd, MOBA_BUFS, HEAD_DIM, MOBA_BLOCK), F32),
            pltpu.VMEM((n_hd, MOBA_BUFS, seq, MOBA_BLOCK), F32),
            pltpu.VMEM((n_hd, MOBA_BUFS, seq, MOBA_BLOCK), BF16),
        ],
        compiler_params=_params("parallel", "parallel"),
        name="moba",
    )(z3, z3, z3, cos_t, sin_t, *stage_w)


def _merge_kernel(ya_ref, yb_ref, gs_ref, ga_ref, h_ref, wglu_ref, wa_ref, wb_ref, wo_ref, o_ref):
    d_ssm = ya_ref.shape[1]
    glu = _dot(ya_ref[...].astype(BF16), wglu_ref[...])
    a = (glu[:, :d_ssm] * jax.nn.sigmoid(glu[:, d_ssm:])).astype(BF16)
    merged = (jax.nn.sigmoid(gs_ref[...]) * _dot(a, wa_ref[...])
              + jax.nn.sigmoid(ga_ref[...]) * _dot(yb_ref[...].astype(BF16), wb_ref[...]))
    o_ref[...] = h_ref[...] + _dot(merged.astype(BF16), wo_ref[...])


def _merge(ya, yb, z, h, wglu, wa, wb, wo, *, tm=256):
    t, d = h.shape
    d_ssm = ya.shape[1]
    d_attn = yb.shape[1]
    gs_blk = (d_ssm + 3 * d_attn) // d
    assert (d_ssm + 3 * d_attn) % d == 0 and t % tm == 0
    return pl.pallas_call(
        _merge_kernel,
        grid=(t // tm,),
        in_specs=[
            pl.BlockSpec((tm, d_ssm), lambda i: (i, 0)),
            pl.BlockSpec((tm, d_attn), lambda i: (i, 0)),
            pl.BlockSpec((tm, d), lambda i: (i, gs_blk)),
            pl.BlockSpec((tm, d), lambda i: (i, gs_blk + 1)),
            pl.BlockSpec((tm, d), lambda i: (i, 0)),
            _resident(wglu.shape), _resident(wa.shape), _resident(wb.shape), _resident(wo.shape),
        ],
        out_specs=pl.BlockSpec((tm, d), lambda i: (i, 0)),
        out_shape=jax.ShapeDtypeStruct((t, d), F32),
        compiler_params=_params("parallel"),
        name="merge",
    )(ya, yb, z, z, h, wglu, wa, wb, wo)


def _ple_kernel(h_ref, p_ref, g_ref, wup_ref, wgate_ref, fg_ref, o_ref, *, final):
    h = h_ref[...]
    gate = jax.nn.sigmoid(_dot(_rms(h, g_ref[...]).astype(BF16), wgate_ref[...].astype(BF16)))
    h = h + _dot(p_ref[...].astype(BF16), wup_ref[...].astype(BF16)) * gate
    if final:
        h = _rms(h, fg_ref[...])
    o_ref[...] = h


def _ple(h, p, g, wup, wgate, fg, layer, *, final, tm=512):
    t, d = h.shape
    dp = p.shape[-1]
    assert t % tm == 0
    layer_resident = lambda *blk: pl.BlockSpec((None,) + blk, lambda i: (layer, 0, 0),
                                               pipeline_mode=pl.Buffered(1))
    return pl.pallas_call(
        functools.partial(_ple_kernel, final=final),
        grid=(t // tm,),
        in_specs=[
            pl.BlockSpec((tm, d), lambda i: (i, 0)),
            pl.BlockSpec((None, tm, dp), lambda i: (layer, i, 0)),
            layer_resident(1, d),
            layer_resident(dp, d), layer_resident(d, d),
            pl.BlockSpec((1, d), lambda i: (0, 0)),
        ],
        out_specs=pl.BlockSpec((tm, d), lambda i: (i, 0)),
        out_shape=jax.ShapeDtypeStruct((t, d), F32),
        compiler_params=_params("parallel"),
        name="ple",
    )(h, p, g, wup, wgate, fg)


def kernel(x, p, ffn1_norm, ffn1_w_gate, ffn1_w_up, ffn1_w_down, mix_norm, w_in, ssm_a_re, ssm_a_im, ssm_log_dt, ssm_b_re, ssm_b_im, ssm_c_re, ssm_c_im, ssm_d, ssm_w_glu, w_branch_ssm, w_branch_attn, w_out, ffn2_norm, ffn2_w_gate, ffn2_w_up, ffn2_w_down, ple_norm, ple_w_up, ple_w_gate, final_norm):
    bsz, seq, d = x.shape
    depth = p.shape[0]
    t = bsz * seq
    d_ssm = ssm_d.shape[-1]
    d_attn = w_branch_attn.shape[-2]
    d_in = w_in.shape[-1]
    cos_t, sin_t = _rotary_tables(seq)
    row = lambda v: v.reshape(1, -1)
    rows = lambda v: v.reshape(depth, 1, -1)
    ffn1_g, ffn2_g, mix_g, ple_g = rows(ffn1_norm), rows(ffn2_norm), rows(mix_norm), rows(ple_norm)
    ssm_w = jax.vmap(_ssm_weights)(ssm_a_re, ssm_a_im, ssm_log_dt, ssm_b_re, ssm_b_im, ssm_c_re, ssm_c_im)
    ssm_skip = rows(ssm_d)
    p2 = p.reshape(depth, t, -1)
    merge_w = (ssm_w_glu, w_branch_ssm, w_branch_attn, w_out)

    h = x.reshape(t, d)
    for i in range(depth):
        h = _ffn(h, ffn1_g, ffn1_w_gate, ffn1_w_up, ffn1_w_down, i)
        z = _in_proj(h, mix_g, w_in, i)
        z3 = z.reshape(bsz, seq, d_in)
        ya = _ssm(z3, ssm_w, ssm_skip, i)
        yb, wglu, wa, wb, wo = _moba(z3, cos_t, sin_t, merge_w, i, d_ssm=d_ssm, d_attn=d_attn)
        h = _merge(ya.reshape(t, d_ssm), yb.reshape(t, d_attn), z, h, wglu, wa, wb, wo)
        h = _ffn(h, ffn2_g, ffn2_w_gate, ffn2_w_up, ffn2_w_down, i)
        h = _ple(h, p2, ple_g, ple_w_up, ple_w_gate, row(final_norm), i, final=(i == depth - 1))
    return h.reshape(bsz, seq, d)
```
